```python
import math
import jax, jax.numpy as jnp
from jax import lax
import numpy as np

D_MODEL = 2048
BATCH = 4
SEQ = 2048
DEPTH = 2
DEC_BATCH = 128
DEC_SEQ = 4
PAST_LEN = 16384
PAGE_SIZE = 128

N_A = (DEPTH + 1) // 2
N_B = DEPTH // 2
CONV_K = 4
EPS = 1e-6
SSD_INNER = 2 * D_MODEL
SSD_HEAD_DIM = 64
SSD_HEADS = SSD_INNER // SSD_HEAD_DIM
SSD_GROUPS = 8
SSD_STATE = 128
SSD_CONV_DIM = SSD_INNER + 2 * SSD_GROUPS * SSD_STATE
SSD_IN_DIM = SSD_INNER + SSD_CONV_DIM + SSD_HEADS
SSD_CHUNK = 128
ML_HEADS = 8
ML_INNER = 2 * D_MODEL
ML_V_DIM = ML_INNER // ML_HEADS
ML_QK_DIM = ML_V_DIM // 2
ML_IN_DIM = 3 * ML_INNER + 2 * ML_HEADS
ML_CHUNK = 64
FFN_DIM = 5632
N_EXPERTS = 8
TOP_K = 2
EXPERT_DIM = 7168
MOE_BLOCK = 256

kernel_name = 'ssd_mlstm_hybrid_step'


def rmsnorm(x, g):
    xf = x.astype(jnp.float32)
    y = xf * lax.rsqrt(jnp.mean(xf * xf, axis=-1, keepdims=True) + EPS)
    return (y * g.astype(jnp.float32)).astype(x.dtype)


def causal_conv(u, buf, w, b):
    L = u.shape[1]
    full = jnp.concatenate([buf.astype(u.dtype), u], axis=1)
    out = b + full[:, CONV_K - 1:CONV_K - 1 + L] * w[CONV_K - 1]
    for k in range(CONV_K - 1):
        out = out + full[:, k:k + L] * w[k]
    return out, full[:, full.shape[1] - (CONV_K - 1):]


def to_chunks(a, c):
    b, L = a.shape[:2]
    return jnp.moveaxis(a.reshape(b, L // c, c, *a.shape[2:]), 1, 0)


def from_chunks(a):
    a = jnp.moveaxis(a, 0, 1)
    return a.reshape(a.shape[0], a.shape[1] * a.shape[2], *a.shape[3:])


def ssd_scan(x, dt, a, bm, cm, h0):
    bsz, L = x.shape[:2]
    c = math.gcd(L, SSD_CHUNK)
    hg = SSD_HEADS // SSD_GROUPS
    xg = x.reshape(bsz, L, SSD_GROUPS, hg, SSD_HEAD_DIM)
    la = (dt * a).reshape(bsz, L, SSD_GROUPS, hg)
    dtg = dt.reshape(bsz, L, SSD_GROUPS, hg)
    mask = jnp.tril(jnp.ones((c, c), dtype=bool))

    def step(h, inp):
        xc, lac, dtc, bc, cc = inp
        cum = jnp.cumsum(lac, axis=1)
        seg = cum[:, :, None] - cum[:, None, :]
        decay = jnp.exp(jnp.where(mask[None, :, :, None, None], seg, -jnp.inf))
        cb = jnp.einsum('btgn,bsgn->btsg', cc, bc)
        w = decay * cb[..., None] * dtc[:, None]
        y = jnp.einsum('btsgj,bsgjp->btgjp', w, xc)
        hgv = h.reshape(bsz, SSD_GROUPS, hg, SSD_HEAD_DIM, SSD_STATE)
        y = y + jnp.exp(cum)[..., None] * jnp.einsum('btgn,bgjpn->btgjp', cc, hgv)
        tail = jnp.exp(cum[:, -1:] - cum) * dtc
        hn = (jnp.exp(cum[:, -1])[..., None, None] * hgv
              + jnp.einsum('bsgj,bsgjp,bsgn->bgjpn', tail, xc, bc))
        return hn.reshape(h.shape), y

    xs = (to_chunks(xg, c), to_chunks(la, c), to_chunks(dtg, c), to_chunks(bm, c), to_chunks(cm, c))
    h, ys = lax.scan(step, h0, xs)
    return from_chunks(ys).reshape(bsz, L, SSD_HEADS, SSD_HEAD_DIM), h


def ssd_mixer(u, h0, conv0, w_in, conv_w, conv_b, dt_bias, a_log, d_skip, norm_g, w_out):
    f32 = jnp.float32
    bsz, L, _ = u.shape
    proj = u @ w_in
    z, xbc, dt_raw = jnp.split(proj, [SSD_INNER, SSD_INNER + SSD_CONV_DIM], axis=-1)
    xbc, conv_new = causal_conv(xbc, conv0, conv_w, conv_b)
    xbc = jax.nn.silu(xbc)
    xs, bm, cm = jnp.split(xbc, [SSD_INNER, SSD_INNER + SSD_GROUPS * SSD_STATE], axis=-1)
    xh = xs.reshape(bsz, L, SSD_HEADS, SSD_HEAD_DIM).astype(f32)
    dt = jax.nn.softplus(dt_raw.astype(f32) + dt_bias.astype(f32))
    a = -jnp.exp(a_log.astype(f32))
    y, h = ssd_scan(xh, dt, a,
                    bm.reshape(bsz, L, SSD_GROUPS, SSD_STATE).astype(f32),
                    cm.reshape(bsz, L, SSD_GROUPS, SSD_STATE).astype(f32),
                    h0.astype(f32))
    y = y + d_skip.astype(f32)[:, None] * xh
    gsz = SSD_INNER // SSD_GROUPS
    y = y.reshape(bsz, L, SSD_GROUPS, gsz) * jax.nn.silu(z.astype(f32)).reshape(bsz, L, SSD_GROUPS, gsz)
    y = y * lax.rsqrt(jnp.mean(y * y, axis=-1, keepdims=True) + EPS)
    y = (y.reshape(bsz, L, SSD_INNER) * norm_g.astype(f32)).astype(u.dtype)
    return y @ w_out, h, conv_new


def mlstm_scan(q, k, v, ig, lf, c0, n0, m0):
    L = q.shape[1]
    c = math.gcd(L, ML_CHUNK)
    mask = jnp.tril(jnp.ones((c, c), dtype=bool))

    def step(carry, inp):
        C, n, m = carry
        qc, kc, vc, ic, fc = inp
        b = jnp.cumsum(fc, axis=1)
        logd = b[:, :, None] - b[:, None, :] + ic[:, None]
        logd = jnp.where(mask[None, :, :, None], logd, -jnp.inf)
        inter = b + m[:, None]
        mt = jnp.maximum(jnp.max(logd, axis=2), inter)
        s = jnp.einsum('bthk,bshk->btsh', qc, kc) * jnp.exp(logd - mt[:, :, None])
        wi = jnp.exp(inter - mt)
        num = (jnp.einsum('btsh,bshv->bthv', s, vc)
               + wi[..., None] * jnp.einsum('bhvk,bthk->bthv', C, qc))
        den = jnp.sum(s, axis=2) + wi * jnp.einsum('bhk,bthk->bth', n, qc)
        h = num / jnp.maximum(jnp.abs(den), jnp.exp(-mt))[..., None]
        g = b[:, -1:] - b + ic
        carry_log = b[:, -1] + m
        m_new = jnp.maximum(carry_log, jnp.max(g, axis=1))
        ws = jnp.exp(g - m_new[:, None])
        wc = jnp.exp(carry_log - m_new)
        C_new = wc[..., None, None] * C + jnp.einsum('bsh,bshv,bshk->bhvk', ws, vc, kc)
        n_new = wc[..., None] * n + jnp.einsum('bsh,bshk->bhk', ws, kc)
        return (C_new, n_new, m_new), h

    xs = (to_chunks(q, c), to_chunks(k, c), to_chunks(v, c), to_chunks(ig, c), to_chunks(lf, c))
    (C, n, m), hs = lax.scan(step, (c0, n0, m0), xs)
    return from_chunks(hs), C, n, m


def mlstm_mixer(u, c0, n0, m0, conv0, w_in, conv_w, conv_b, w_q, w_k, b_i, b_f, norm_g, w_out):
    f32 = jnp.float32
    bsz, L, _ = u.shape
    proj = u @ w_in
    xc, v, o, gates = jnp.split(proj, [ML_INNER, 2 * ML_INNER, 3 * ML_INNER], axis=-1)
    xc, conv_new = causal_conv(xc, conv0, conv_w, conv_b)
    xc = jax.nn.silu(xc).reshape(bsz, L, ML_HEADS, ML_V_DIM)
    q = jnp.einsum('blhd,hdk->blhk', xc, w_q).astype(f32) * (ML_QK_DIM ** -0.5)
    k = jnp.einsum('blhd,hdk->blhk', xc, w_k).astype(f32)
    v = v.reshape(bsz, L, ML_HEADS, ML_V_DIM).astype(f32)
    gates = gates.astype(f32)
    ig = gates[..., :ML_HEADS] + b_i.astype(f32)
    lf = jax.nn.log_sigmoid(gates[..., ML_HEADS:] + b_f.astype(f32))
    h, C, n, m = mlstm_scan(q, k, v, ig, lf, c0.astype(f32), n0.astype(f32), m0.astype(f32))
    h = h * lax.rsqrt(jnp.mean(h * h, axis=-1, keepdims=True) + EPS)
    h = h.reshape(bsz, L, ML_INNER) * norm_g.astype(f32) * jax.nn.sigmoid(o.astype(f32))
    return h.astype(u.dtype) @ w_out, C, n, m, conv_new


def swiglu(x, w_gate, w_up, w_down):
    return (jax.nn.silu(x @ w_gate) * (x @ w_up)) @ w_down


def moe_swiglu(x, w_router, b_router, w_gate, w_up, w_down):
    f32 = jnp.float32
    bsz, L, D = x.shape
    T = bsz * L
    xt = x.reshape(T, D)
    logits = (xt @ w_router).astype(f32) + b_router.astype(f32)
    top_logit, top_e = lax.top_k(logits, TOP_K)
    gates = jax.nn.softmax(top_logit, axis=-1)
    e_flat = top_e.reshape(-1)
    tok_flat = jnp.repeat(jnp.arange(T, dtype=jnp.int32), TOP_K)
    g_flat = gates.reshape(-1)
    order = jnp.argsort(e_flat)
    e_s, tok_s, g_s = e_flat[order], tok_flat[order], g_flat[order]
    counts = jnp.bincount(e_flat, length=N_EXPERTS)
    padded = (counts + MOE_BLOCK - 1) // MOE_BLOCK * MOE_BLOCK
    starts = jnp.cumsum(counts) - counts
    pends = jnp.cumsum(padded)
    pstarts = pends - padded
    dest = pstarts[e_s] + jnp.arange(T * TOP_K) - starts[e_s]
    n_blocks = -(-(T * TOP_K + N_EXPERTS * (MOE_BLOCK - 1)) // MOE_BLOCK)
    P = n_blocks * MOE_BLOCK
    slot_tok = jnp.zeros((P,), jnp.int32).at[dest].set(tok_s)
    slot_g = jnp.zeros((P,), f32).at[dest].set(g_s)
    block_e = jnp.minimum(jnp.searchsorted(pends, jnp.arange(n_blocks) * MOE_BLOCK, side='right'),
                          N_EXPERTS - 1)
    xb = xt[slot_tok].reshape(n_blocks, MOE_BLOCK, D)

    def expert_block(args):
        xblk, e = args
        return swiglu(xblk, w_gate[e], w_up[e], w_down[e])

    yb = lax.map(expert_block, (xb, block_e)).reshape(P, D)
    y = jnp.zeros((T, D), f32).at[slot_tok].add(yb.astype(f32) * slot_g[:, None])
    return y.astype(x.dtype).reshape(bsz, L, D)


def run_trunk(x, ssm_h, ssm_conv, ml_c, ml_n, ml_m, ml_conv, p):
    new_h, new_sc, new_c, new_n, new_m, new_mc = [], [], [], [], [], []
    for i in range(DEPTH):
        j = i // 2
        if i % 2 == 0:
            y, h, sc = ssd_mixer(rmsnorm(x, p['norm_mix_a'][j]), ssm_h[j], ssm_conv[j],
                                 p['ssd_w_in'][j], p['ssd_conv_w'][j], p['ssd_conv_b'][j],
                                 p['ssd_dt_bias'][j], p['ssd_a_log'][j], p['ssd_d'][j],
                                 p['ssd_norm'][j], p['ssd_w_out'][j])
            x = x + y
            x = x + swiglu(rmsnorm(x, p['norm_ffn_a'][j]), p['ffn_w_gate'][j],
                           p['ffn_w_up'][j], p['ffn_w_down'][j])
            new_h.append(h)
            new_sc.append(sc)
        else:
            y, C, n, m, mc = mlstm_mixer(rmsnorm(x, p['norm_mix_b'][j]), ml_c[j], ml_n[j], ml_m[j],
                                         ml_conv[j], p['ml_w_in'][j], p['ml_conv_w'][j],
                                         p['ml_conv_b'][j], p['ml_w_q'][j], p['ml_w_k'][j],
                                         p['ml_b_i'][j], p['ml_b_f'][j], p['ml_norm'][j],
                                         p['ml_w_out'][j])
            x = x + y
            x = x + moe_swiglu(rmsnorm(x, p['norm_ffn_b'][j]), p['moe_w_router'][j],
                               p['moe_b_router'][j], p['moe_w_gate'][j], p['moe_w_up'][j],
                               p['moe_w_down'][j])
            new_c.append(C)
            new_n.append(n)
            new_m.append(m)
            new_mc.append(mc)
    return (rmsnorm(x, p['final_norm']), jnp.stack(new_h), jnp.stack(new_sc), jnp.stack(new_c),
            jnp.stack(new_n), jnp.stack(new_m), jnp.stack(new_mc))


def _normal(k, shape, scale):
    return scale * jax.random.normal(k, shape, jnp.float32)


def _gain(k, shape):
    return 1.0 + 0.02 * jax.random.normal(k, shape, jnp.float32)


def setup_inputs(seed: int = 0) -> dict:
    key = jax.random.key(seed)
    k = jax.random.split(key, 40)
    D = D_MODEL
    dt0 = jnp.exp(jax.random.uniform(k[12], (N_A, SSD_HEADS), jnp.float32)
                  * (math.log(0.1) - math.log(0.001)) + math.log(0.001))
    return {
        'x_prompt': _normal(k[0], (BATCH, SEQ, D), 1.0),
        'x_sample': _normal(k[1], (DEC_BATCH, DEC_SEQ, D), 1.0),
        'state_ssm': _normal(k[2], (N_A, DEC_BATCH, SSD_HEADS, SSD_HEAD_DIM, SSD_STATE), 0.1),
        'state_ssm_conv': _normal(k[3], (N_A, DEC_BATCH, CONV_K - 1, SSD_CONV_DIM), 1.0),
        'state_mlstm_c': _normal(k[4], (N_B, DEC_BATCH, ML_HEADS, ML_V_DIM, ML_QK_DIM), 0.1),
        'state_mlstm_n': _normal(k[5], (N_B, DEC_BATCH, ML_HEADS, ML_QK_DIM), 0.1),
        'state_mlstm_m': _normal(k[6], (N_B, DEC_BATCH, ML_HEADS), 1.0),
        'state_mlstm_conv': _normal(k[7], (N_B, DEC_BATCH, CONV_K - 1, ML_INNER), 1.0),
        'norm_mix_a': _gain(k[8], (N_A, D)),
        'ssd_w_in': _normal(k[9], (N_A, D, SSD_IN_DIM), D ** -0.5),
        'ssd_conv_w': _normal(k[10], (N_A, CONV_K, SSD_CONV_DIM), CONV_K ** -0.5),
        'ssd_conv_b': _normal(k[11], (N_A, SSD_CONV_DIM), 0.01),
        'ssd_dt_bias': dt0 + jnp.log(-jnp.expm1(-dt0)),
        'ssd_a_log': jnp.log(jax.random.uniform(k[13], (N_A, SSD_HEADS), jnp.float32, 1.0, 16.0)),
        'ssd_d': _gain(k[14], (N_A, SSD_HEADS)),
        'ssd_norm': _gain(k[15], (N_A, SSD_INNER)),
        'ssd_w_out': _normal(k[16], (N_A, SSD_INNER, D), SSD_INNER ** -0.5),
        'norm_ffn_a': _gain(k[17], (N_A, D)),
        'ffn_w_gate': _normal(k[18], (N_A, D, FFN_DIM), D ** -0.5),
        'ffn_w_up': _normal(k[19], (N_A, D, FFN_DIM), D ** -0.5),
        'ffn_w_down': _normal(k[20], (N_A, FFN_DIM, D), FFN_DIM ** -0.5),
        'norm_mix_b': _gain(k[21], (N_B, D)),
        'ml_w_in': _normal(k[22], (N_B, D, ML_IN_DIM), D ** -0.5),
        'ml_conv_w': _normal(k[23], (N_B, CONV_K, ML_INNER), CONV_K ** -0.5),
        'ml_conv_b': _normal(k[24], (N_B, ML_INNER), 0.01),
        'ml_w_q': _normal(k[25], (N_B, ML_HEADS, ML_V_DIM, ML_QK_DIM), ML_V_DIM ** -0.5),
        'ml_w_k': _normal(k[26], (N_B, ML_HEADS, ML_V_DIM, ML_QK_DIM), ML_V_DIM ** -0.5),
        'ml_b_i': _normal(k[27], (N_B, ML_HEADS), 0.1),
        'ml_b_f': 3.0 + 3.0 * jax.random.uniform(k[28], (N_B, ML_HEADS), jnp.float32),
        'ml_norm': _gain(k[29], (N_B, ML_INNER)),
        'ml_w_out': _normal(k[30], (N_B, ML_INNER, D), ML_INNER ** -0.5),
        'norm_ffn_b': _gain(k[31], (N_B, D)),
        'moe_w_router': _normal(k[32], (N_B, D, N_EXPERTS), D ** -0.5),
        'moe_b_router': _normal(k[33], (N_B, N_EXPERTS), 0.01),
        'moe_w_gate': _normal(k[34], (N_B, N_EXPERTS, D, EXPERT_DIM), D ** -0.5),
        'moe_w_up': _normal(k[35], (N_B, N_EXPERTS, D, EXPERT_DIM), D ** -0.5),
        'moe_w_down': _normal(k[36], (N_B, N_EXPERTS, EXPERT_DIM, D), EXPERT_DIM ** -0.5),
        'final_norm': _gain(k[37], (D,)),
    }


def reference(x_prompt, x_sample, state_ssm, state_ssm_conv, state_mlstm_c, state_mlstm_n,
              state_mlstm_m, state_mlstm_conv, norm_mix_a, ssd_w_in, ssd_conv_w, ssd_conv_b,
              ssd_dt_bias, ssd_a_log, ssd_d, ssd_norm, ssd_w_out, norm_ffn_a, ffn_w_gate, ffn_w_up,
              ffn_w_down, norm_mix_b, ml_w_in, ml_conv_w, ml_conv_b, ml_w_q, ml_w_k, ml_b_i, ml_b_f,
              ml_norm, ml_w_out, norm_ffn_b, moe_w_router, moe_b_router, moe_w_gate, moe_w_up,
              moe_w_down, final_norm):
    p = dict(norm_mix_a=norm_mix_a, ssd_w_in=ssd_w_in, ssd_conv_w=ssd_conv_w, ssd_conv_b=ssd_conv_b,
             ssd_dt_bias=ssd_dt_bias, ssd_a_log=ssd_a_log, ssd_d=ssd_d, ssd_norm=ssd_norm,
             ssd_w_out=ssd_w_out, norm_ffn_a=norm_ffn_a, ffn_w_gate=ffn_w_gate, ffn_w_up=ffn_w_up,
             ffn_w_down=ffn_w_down, norm_mix_b=norm_mix_b, ml_w_in=ml_w_in, ml_conv_w=ml_conv_w,
             ml_conv_b=ml_conv_b, ml_w_q=ml_w_q, ml_w_k=ml_w_k, ml_b_i=ml_b_i, ml_b_f=ml_b_f,
             ml_norm=ml_norm, ml_w_out=ml_w_out, norm_ffn_b=norm_ffn_b, moe_w_router=moe_w_router,
             moe_b_router=moe_b_router, moe_w_gate=moe_w_gate, moe_w_up=moe_w_up,
             moe_w_down=moe_w_down, final_norm=final_norm)
    f32 = jnp.float32
    bp = x_prompt.shape[0]
    z_ssm = jnp.zeros((N_A, bp, SSD_HEADS, SSD_HEAD_DIM, SSD_STATE), f32)
    z_ssm_conv = jnp.zeros((N_A, bp, CONV_K - 1, SSD_CONV_DIM), x_prompt.dtype)
    z_c = jnp.zeros((N_B, bp, ML_HEADS, ML_V_DIM, ML_QK_DIM), f32)
    z_n = jnp.zeros((N_B, bp, ML_HEADS, ML_QK_DIM), f32)
    z_m = jnp.zeros((N_B, bp, ML_HEADS), f32)
    z_mconv = jnp.zeros((N_B, bp, CONV_K - 1, ML_INNER), x_prompt.dtype)
    y_prompt, ssm_p, ssm_conv_p, c_p, n_p, m_p, mconv_p = run_trunk(
        x_prompt, z_ssm, z_ssm_conv, z_c, z_n, z_m, z_mconv, p)
    y_sample, ssm_s, ssm_conv_s, c_s, n_s, m_s, mconv_s = run_trunk(
        x_sample, state_ssm, state_ssm_conv, state_mlstm_c, state_mlstm_n, state_mlstm_m,
        state_mlstm_conv, p)
    return (y_prompt, y_sample, ssm_p, ssm_conv_p, c_p, n_p, m_p, mconv_p,
            ssm_s, ssm_conv_s, c_s, n_s, m_s, mconv_s)
```

```python
import functools

import jax
import jax.numpy as jnp
from jax import lax
from jax.experimental import pallas as pl
from jax.experimental.pallas import tpu as pltpu

F32, BF16, I32 = jnp.float32, jnp.bfloat16, jnp.int32

EPS = 1e-6
D_MODEL = 2048
SSD_INNER = 4096
SSD_HEAD_DIM = 64
SSD_GROUPS = 8
SSD_STATE = 128
SSD_GROUP_DIM = SSD_INNER // SSD_GROUPS
SSD_HEADS_PER_GROUP = SSD_GROUP_DIM // SSD_HEAD_DIM
SSD_XBC = SSD_INNER + 2 * SSD_GROUPS * SSD_STATE
ML_HEADS = 8
ML_INNER = 4096
ML_V_DIM = 512
ML_QK_DIM = 256
N_EXPERTS = 8
TOP_K = 2

LANES = 128
SUBLANES = 8
SSD_CHUNK = 128
ML_CHUNK = 256
SAMPLE_LEN = 4
SEQS_PER_STEP = SUBLANES // SAMPLE_LEN
ROW_TILE = 512
MOE_ROWS = 256
NEG = -1e30
VMEM_CAP = 60 << 20


def _vmem_params(nbytes, semantics):
    return pltpu.CompilerParams(dimension_semantics=semantics,
                                vmem_limit_bytes=int(min(nbytes + (8 << 20), VMEM_CAP)))


def _softplus(x):
    return jnp.maximum(x, 0.0) + jnp.log1p(jnp.exp(-jnp.abs(x)))


def _silu(x):
    return x * jax.nn.sigmoid(x)


def _cumsum_rows(x):
    n = x.shape[0]
    row = lax.broadcasted_iota(I32, x.shape, 0)
    s = 1
    while s < n:
        x = x + jnp.where(row >= s, pltpu.roll(x, s, axis=0), 0.0)
        s *= 2
    return x


def _lane_col(x, lane_idx):
    lane = lax.broadcasted_iota(I32, x.shape, 1)
    return jnp.sum(jnp.where(lane == lane_idx, x, 0.0), axis=1, keepdims=True)


def _col_to_row(col, eye):
    return jnp.sum(jnp.where(eye, col, 0.0), axis=0, keepdims=True)


def _conv_silu(u, prev, w, b):
    c, width = u.shape
    row8 = lax.broadcasted_iota(I32, (SUBLANES, width), 0)
    acc = b + u * w[3:4]
    for d in (1, 2, 3):
        sh = pltpu.roll(u, d, axis=0)
        head = jnp.where(row8 < d, pltpu.roll(prev, d, axis=0), sh[:SUBLANES])
        sh = head if c == SUBLANES else jnp.concatenate([head, sh[SUBLANES:]], axis=0)
        acc = acc + sh * w[3 - d:4 - d]
    return _silu(acc)


def _conv_silu_sample(u, prev, w, b, pos):
    acc = b + u * w[3:4]
    for d in (1, 2, 3):
        sh = jnp.where(pos >= d, pltpu.roll(u, d, axis=0), pltpu.roll(prev, d + SAMPLE_LEN, axis=0))
        acc = acc + sh * w[3 - d:4 - d]
    return _silu(acc)


def _dot_t(a, b):
    return lax.dot_general(a, b, (((1,), (1,)), ((), ())), preferred_element_type=F32)


def _tdot(a, b):
    return lax.dot_general(a, b, (((0,), (0,)), ((), ())), preferred_element_type=F32)


def _rmsnorm_body(x_ref, g_ref, o_ref):
    x = x_ref[...]
    y = x * lax.rsqrt(jnp.mean(x * x, axis=-1, keepdims=True) + EPS)
    o_ref[...] = (y * g_ref[...]).astype(o_ref.dtype)


def _rmsnorm(x, g, out_dtype):
    t, d = x.shape
    return pl.pallas_call(
        _rmsnorm_body, grid=(t // ROW_TILE,),
        in_specs=[pl.BlockSpec((ROW_TILE, d), lambda i: (i, 0)), pl.BlockSpec((1, d), lambda i: (0, 0))],
        out_specs=pl.BlockSpec((ROW_TILE, d), lambda i: (i, 0)),
        out_shape=jax.ShapeDtypeStruct((t, d), out_dtype), name="rmsnorm",
    )(x, g.reshape(1, d))


def _matmul_body(be_ref, rc_ref, nu_ref, a_ref, *rest, n_w, has_scale, has_res):
    del be_ref
    w_refs, rest = rest[:n_w], rest[n_w:]
    scale_ref = rest[0] if has_scale else None
    rest = rest[1:] if has_scale else rest
    res_ref = rest[0] if has_res else None
    rest = rest[1:] if has_res else rest
    o_ref, wb_refs = rest[0], rest[1:]
    b = pl.program_id(1)

    @pl.when(rc_ref[b] == 1)
    def _recast():
        for w_ref, wb_ref in zip(w_refs, wb_refs):
            wb_ref[...] = w_ref[...].astype(BF16)

    @pl.when(b < nu_ref[0])
    def _compute():
        a = a_ref[...]
        acc = jnp.dot(a, wb_refs[0][...], preferred_element_type=F32)
        if n_w == 2:
            acc = _silu(acc) * jnp.dot(a, wb_refs[1][...], preferred_element_type=F32)
        if has_scale:
            acc = acc * scale_ref[...]
        if has_res:
            acc = acc + res_ref[...]
        o_ref[...] = acc.astype(o_ref.dtype)

    @pl.when(b >= nu_ref[0])
    def _unused():
        o_ref[...] = jnp.zeros(o_ref.shape, o_ref.dtype)


def _matmul(a, ws, *, tm, tn, n_cols, out_dtype, name, block_e=None, recast=None, n_used=None,
            scale=None, res=None):
    m, k = a.shape
    nb, nj, n_w = m // tm, n_cols // tn, len(ws)
    if block_e is None:
        block_e = jnp.zeros((nb,), I32)
        recast = jnp.zeros((nb,), I32).at[0].set(1)
        n_used = jnp.full((1,), nb, I32)
    in_specs = [pl.BlockSpec((tm, k), lambda j, b, be, rc, nu: (b, 0))]
    in_specs += [pl.BlockSpec((None, k, tn), lambda j, b, be, rc, nu: (be[b], 0, j))] * n_w
    args = [a, *ws]
    if scale is not None:
        in_specs.append(pl.BlockSpec((tm, 1), lambda j, b, be, rc, nu: (b, 0)))
        args.append(scale)
    if res is not None:
        in_specs.append(pl.BlockSpec((tm, tn), lambda j, b, be, rc, nu: (b, j)))
        args.append(res)
    out_bytes = jnp.dtype(out_dtype).itemsize
    vmem = (n_w * k * tn * (2 * 4 + 2) + 2 * tm * k * 2 + 2 * tm * tn * out_bytes
            + (2 * tm * tn * 4 if res is not None else 0) + 2 * tm * tn * 4)
    grid_spec = pltpu.PrefetchScalarGridSpec(
        num_scalar_prefetch=3, grid=(nj, nb), in_specs=in_specs,
        out_specs=pl.BlockSpec((tm, tn), lambda j, b, be, rc, nu: (b, j)),
        scratch_shapes=[pltpu.VMEM((k, tn), BF16)] * n_w)
    body = functools.partial(_matmul_body, n_w=n_w, has_scale=scale is not None, has_res=res is not None)
    return pl.pallas_call(
        body, grid_spec=grid_spec, out_shape=jax.ShapeDtypeStruct((m, n_cols), out_dtype), name=name,
        compiler_params=_vmem_params(vmem, ("arbitrary", "arbitrary")),
    )(block_e, recast, n_used, *args)


def _narrow_body(a_ref, w_ref, o_ref):
    o_ref[...] = jnp.dot(a_ref[...], w_ref[...].astype(BF16), preferred_element_type=F32)


def _narrow_matmul(a, w, name):
    t, k = a.shape
    wp = jnp.pad(w, ((0, 0), (0, LANES - w.shape[1])))
    return pl.pallas_call(
        _narrow_body, grid=(t // ROW_TILE,),
        in_specs=[pl.BlockSpec((ROW_TILE, k), lambda i: (i, 0)), pl.BlockSpec((k, LANES), lambda i: (0, 0))],
        out_specs=pl.BlockSpec((ROW_TILE, LANES), lambda i: (i, 0)),
        out_shape=jax.ShapeDtypeStruct((t, LANES), F32), name=name,
    )(a, wp)


def _router_body(a_ref, w_ref, b_ref, e_ref, g_ref):
    logits = jnp.dot(a_ref[...], w_ref[...].astype(BF16), preferred_element_type=F32) + b_ref[...]
    lane = lax.broadcasted_iota(I32, logits.shape, 1)
    lane_f = lane.astype(F32)
    m1 = jnp.max(logits, axis=1, keepdims=True)
    i1 = jnp.min(jnp.where(logits == m1, lane_f, float(LANES)), axis=1, keepdims=True)
    rest = jnp.where(lane_f == i1, 2 * NEG, logits)
    m2 = jnp.max(rest, axis=1, keepdims=True)
    i2 = jnp.min(jnp.where(rest == m2, lane_f, float(LANES)), axis=1, keepdims=True)
    e2 = jnp.exp(m2 - m1)
    den = 1.0 + e2
    e_ref[...] = jnp.where(lane == 0, i1, jnp.where(lane == 1, i2, 0.0)).astype(I32)
    g_ref[...] = jnp.where(lane == 0, 1.0 / den, jnp.where(lane == 1, e2 / den, 0.0))


def _router(a, w, b):
    t, k = a.shape
    wp = jnp.pad(w, ((0, 0), (0, LANES - N_EXPERTS)))
    bp = jnp.pad(b.reshape(1, N_EXPERTS), ((0, 0), (0, LANES - N_EXPERTS)), constant_values=NEG)
    e, g = pl.pallas_call(
        _router_body, grid=(t // ROW_TILE,),
        in_specs=[pl.BlockSpec((ROW_TILE, k), lambda i: (i, 0)), pl.BlockSpec((k, LANES), lambda i: (0, 0)),
                  pl.BlockSpec((1, LANES), lambda i: (0, 0))],
        out_specs=[pl.BlockSpec((ROW_TILE, LANES), lambda i: (i, 0))] * 2,
        out_shape=[jax.ShapeDtypeStruct((t, LANES), I32), jax.ShapeDtypeStruct((t, LANES), F32)],
        name="moe_router",
    )(a, wp, bp)
    return e[:, :TOP_K], g[:, :TOP_K]


def _ssd_prompt_body(z_ref, xr_ref, br_ref, cr_ref, dt_ref, cwx_ref, cwb_ref, cwc_ref, cbx_ref, cbb_ref,
                     cbc_ref, dtb_ref, alog_ref, dsk_ref, ng_ref, y_ref, h_ref, px_ref, pb_ref, pc_ref):
    c = xr_ref.shape[0]
    k, g = pl.program_id(1), pl.program_id(2)

    @pl.when(k == 0)
    def _init():
        h_ref[0, g] = jnp.zeros((SSD_GROUP_DIM, SSD_STATE), F32)
        px_ref[g] = jnp.zeros((SUBLANES, SSD_GROUP_DIM), F32)
        pb_ref[g] = jnp.zeros((SUBLANES, SSD_STATE), F32)
        pc_ref[g] = jnp.zeros((SUBLANES, SSD_STATE), F32)

    xr, br, cr = xr_ref[...], br_ref[...], cr_ref[...]
    xs = _conv_silu(xr, px_ref[g], cwx_ref[...], cbx_ref[...])
    bm = _conv_silu(br, pb_ref[g], cwb_ref[...], cbb_ref[...])
    cm = _conv_silu(cr, pc_ref[g], cwc_ref[...], cbc_ref[...])
    px_ref[g] = xr[c - SUBLANES:]
    pb_ref[g] = br[c - SUBLANES:]
    pc_ref[g] = cr[c - SUBLANES:]

    dt = _softplus(dt_ref[...] + dtb_ref[...])
    cum = _cumsum_rows(dt * (-jnp.exp(alog_ref[...])))

    row = lax.broadcasted_iota(I32, (c, c), 0)
    col = lax.broadcasted_iota(I32, (c, c), 1)
    eye, causal = row == col, col <= row
    lane = lax.broadcasted_iota(I32, (c, LANES), 1)
    low = lane < SSD_HEAD_DIM
    hrow = lax.broadcasted_iota(I32, (LANES, SSD_STATE), 0) < SSD_HEAD_DIM
    bm_bf, cm_bf = bm.astype(BF16), cm.astype(BF16)
    cb = _dot_t(cm_bf, bm_bf)

    ys = []
    for p in range(SSD_HEADS_PER_GROUP // 2):
        ws, ecols, tcols, elast = [], [], [], []
        for j in (2 * p, 2 * p + 1):
            head = g * SSD_HEADS_PER_GROUP + j
            ccol = _lane_col(cum, head)
            dcol = _lane_col(dt, head)
            crow, drow = _col_to_row(ccol, eye), _col_to_row(dcol, eye)
            w = jnp.where(causal, jnp.exp(jnp.where(causal, ccol - crow, NEG)), 0.0) * cb * drow
            ws.append(w.astype(BF16))
            last = ccol[c - 1:c]
            ecols.append(jnp.exp(ccol))
            tcols.append(jnp.exp(last - ccol) * dcol)
            elast.append(jnp.exp(last))
        xp = xs[:, p * LANES:(p + 1) * LANES]
        rhs = jnp.concatenate([jnp.where(low, xp, 0.0), jnp.where(low, 0.0, xp)], axis=0).astype(BF16)
        y_intra = jnp.dot(jnp.concatenate(ws, axis=1), rhs, preferred_element_type=F32)
        hp = h_ref[0, g, p * LANES:(p + 1) * LANES, :]
        y_inter = _dot_t(cm_bf, hp.astype(BF16)) * jnp.where(low, ecols[0], ecols[1])
        ys.append(y_intra + y_inter)
        xw = (xp * jnp.where(low, tcols[0], tcols[1])).astype(BF16)
        h_ref[0, g, p * LANES:(p + 1) * LANES, :] = (
            jnp.where(hrow, elast[0], elast[1]) * hp + _tdot(xw, bm_bf))

    y = jnp.concatenate(ys, axis=1) + dsk_ref[...] * xs
    y = y * _silu(z_ref[...])
    y = y * lax.rsqrt(jnp.mean(y * y, axis=1, keepdims=True) + EPS)
    y_ref[...] = (y * ng_ref[...]).astype(y_ref.dtype)


def _ssd_prompt(proj, dt_raw, p, n_seq, seq_len):
    c = SSD_CHUNK
    nc = seq_len // c
    gd, st = SSD_GROUP_DIM, SSD_STATE
    xo, bo, co = SSD_INNER // gd, 2 * SSD_INNER // st, (2 * SSD_INNER + SSD_GROUPS * st) // st
    cxo, cbo, cco = 0, SSD_INNER // st, (SSD_INNER + SSD_GROUPS * st) // st

    def rows(f):
        return lambda b, k, g: (b * nc + k, f(g))

    in_specs = [
        pl.BlockSpec((c, gd), rows(lambda g: g)),
        pl.BlockSpec((c, gd), rows(lambda g: xo + g)),
        pl.BlockSpec((c, st), rows(lambda g: bo + g)),
        pl.BlockSpec((c, st), rows(lambda g: co + g)),
        pl.BlockSpec((c, LANES), rows(lambda g: 0)),
        pl.BlockSpec((4, gd), lambda b, k, g: (0, cxo + g)),
        pl.BlockSpec((4, st), lambda b, k, g: (0, cbo + g)),
        pl.BlockSpec((4, st), lambda b, k, g: (0, cco + g)),
        pl.BlockSpec((1, gd), lambda b, k, g: (0, cxo + g)),
        pl.BlockSpec((1, st), lambda b, k, g: (0, cbo + g)),
        pl.BlockSpec((1, st), lambda b, k, g: (0, cco + g)),
        pl.BlockSpec((1, LANES), lambda b, k, g: (0, 0)),
        pl.BlockSpec((1, LANES), lambda b, k, g: (0, 0)),
        pl.BlockSpec((1, gd), lambda b, k, g: (0, g)),
        pl.BlockSpec((1, gd), lambda b, k, g: (0, g)),
    ]
    return pl.pallas_call(
        _ssd_prompt_body, grid=(n_seq, nc, SSD_GROUPS), in_specs=in_specs,
        out_specs=[pl.BlockSpec((c, gd), lambda b, k, g: (b * nc + k, g)),
                   pl.BlockSpec((1, SSD_GROUPS, gd, st), lambda b, k, g: (b, 0, 0, 0))],
        out_shape=[jax.ShapeDtypeStruct((n_seq * seq_len, SSD_INNER), BF16),
                   jax.ShapeDtypeStruct((n_seq, SSD_GROUPS, gd, st), F32)],
        scratch_shapes=[pltpu.VMEM((SSD_GROUPS, SUBLANES, gd), F32), pltpu.VMEM((SSD_GROUPS, SUBLANES, st), F32),
                        pltpu.VMEM((SSD_GROUPS, SUBLANES, st), F32)],
        compiler_params=pltpu.CompilerParams(dimension_semantics=("arbitrary",) * 3),
        name="ssd_prompt",
    )(proj, proj, proj, proj, dt_raw, p["conv_w"], p["conv_w"], p["conv_w"], p["conv_b"], p["conv_b"],
      p["conv_b"], p["dt_bias"], p["a_log"], p["d_lanes"], p["norm"])


def _expand_heads(v, first_head, n_heads):
    lane = lax.broadcasted_iota(I32, (v.shape[0], LANES), 1)
    low = lane < SSD_HEAD_DIM
    out = []
    for q in range(n_heads // 2):
        j = first_head + 2 * q
        out.append(jnp.where(low, v[:, j:j + 1], v[:, j + 1:j + 2]))
    return jnp.concatenate(out, axis=1)


def _ssd_sample_body(z_ref, xr_ref, br_ref, cr_ref, dt_ref, prevx_ref, prevb_ref, prevc_ref, cwx_ref, cwb_ref,
                     cwc_ref, cbx_ref, cbb_ref, cbc_ref, dtb_ref, alog_ref, dsk_ref, ng_ref, h0_ref,
                     y_ref, h_ref):
    n = SUBLANES
    row1 = lax.broadcasted_iota(I32, (n, 1), 0)
    pos1, first1 = row1 % SAMPLE_LEN, row1 < SAMPLE_LEN

    def src(v, s):
        return jnp.where(first1, v[s:s + 1], v[SAMPLE_LEN + s:SAMPLE_LEN + s + 1])

    xs = _conv_silu_sample(xr_ref[...], prevx_ref[...], cwx_ref[...], cbx_ref[...], pos1)
    bm = _conv_silu_sample(br_ref[...], prevb_ref[...], cwb_ref[...], cbb_ref[...], pos1)
    cm = _conv_silu_sample(cr_ref[...], prevc_ref[...], cwc_ref[...], cbc_ref[...], pos1)
    dt = _softplus(dt_ref[...] + dtb_ref[...])
    la = dt * (-jnp.exp(alog_ref[...]))
    cum = la + jnp.where(pos1 >= 1, pltpu.roll(la, 1, axis=0), 0.0)
    cum = cum + jnp.where(pos1 >= 2, pltpu.roll(cum, 2, axis=0), 0.0)
    z = z_ref[...]
    pad_rows = LANES - n
    hrow = lax.broadcasted_iota(I32, (SSD_GROUP_DIM, SSD_STATE), 0) // SSD_HEAD_DIM

    ys = []
    for g in range(SSD_GROUPS):
        gs = slice(g * SSD_GROUP_DIM, (g + 1) * SSD_GROUP_DIM)
        ns = slice(g * SSD_STATE, (g + 1) * SSD_STATE)
        x_g, b_g, c_g = xs[:, gs], bm[:, ns], cm[:, ns]
        cum_e = _expand_heads(cum, g * SSD_HEADS_PER_GROUP, SSD_HEADS_PER_GROUP)
        dt_e = _expand_heads(dt, g * SSD_HEADS_PER_GROUP, SSD_HEADS_PER_GROUP)
        last_e = src(cum_e, SAMPLE_LEN - 1)
        y = jnp.zeros((n, SSD_GROUP_DIM), F32)
        for s in range(SAMPLE_LEN):
            cb_s = jnp.sum(c_g * src(b_g, s), axis=1, keepdims=True)
            decay = jnp.exp(jnp.where(pos1 >= s, cum_e - src(cum_e, s), NEG))
            y = y + decay * cb_s * src(dt_e, s) * src(x_g, s)
        c_bf = c_g.astype(BF16)
        h_old = [h0_ref[q, g] for q in range(SEQS_PER_STEP)]
        y_inter = jnp.where(first1, _dot_t(c_bf, h_old[0].astype(BF16)), _dot_t(c_bf, h_old[1].astype(BF16)))
        y = y + y_inter * jnp.exp(cum_e) + dsk_ref[:, gs] * x_g
        y = y * _silu(z[:, gs])
        y = y * lax.rsqrt(jnp.mean(y * y, axis=1, keepdims=True) + EPS)
        ys.append(y * ng_ref[:, gs])
        xw = x_g * jnp.exp(last_e - cum_e) * dt_e
        b_pad = jnp.concatenate([b_g, jnp.zeros((pad_rows, SSD_STATE), F32)], axis=0).astype(BF16)
        for q in range(SEQS_PER_STEP):
            own = (row1 >= q * SAMPLE_LEN) & (row1 < (q + 1) * SAMPLE_LEN)
            xw_pad = jnp.concatenate([jnp.where(own, xw, 0.0), jnp.zeros((pad_rows, SSD_GROUP_DIM), F32)],
                                     axis=0).astype(BF16)
            last_row = (q + 1) * SAMPLE_LEN - 1
            elast = jnp.exp(cum[last_row:last_row + 1])
            dec = jnp.zeros((SSD_GROUP_DIM, SSD_STATE), F32)
            for j in range(SSD_HEADS_PER_GROUP):
                hd = g * SSD_HEADS_PER_GROUP + j
                dec = jnp.where(hrow == j, elast[:, hd:hd + 1], dec)
            h_ref[q, g] = dec * h_old[q] + _tdot(xw_pad, b_pad)
    y_ref[...] = jnp.concatenate(ys, axis=1)


def _ssd_sample(proj, dt_raw, prev, h0, p, row0, n_seq):
    n = SUBLANES
    steps = n_seq // SEQS_PER_STEP
    r0 = row0 // n
    st = SSD_STATE
    zo, xo, bo, co = 0, 1, 2 * SSD_INNER // (SSD_GROUPS * st), (2 * SSD_INNER + SSD_GROUPS * st) // (SSD_GROUPS * st)
    gw = SSD_GROUPS * st
    in_specs = [
        pl.BlockSpec((n, SSD_INNER), lambda i: (r0 + i, zo)),
        pl.BlockSpec((n, SSD_INNER), lambda i: (r0 + i, xo)),
        pl.BlockSpec((n, gw), lambda i: (r0 + i, bo)),
        pl.BlockSpec((n, gw), lambda i: (r0 + i, co)),
        pl.BlockSpec((n, LANES), lambda i: (r0 + i, 0)),
        pl.BlockSpec((n, SSD_INNER), lambda i: (i, 0)),
        pl.BlockSpec((n, gw), lambda i: (i, SSD_INNER // gw)),
        pl.BlockSpec((n, gw), lambda i: (i, SSD_INNER // gw + 1)),
        pl.BlockSpec((4, SSD_INNER), lambda i: (0, 0)),
        pl.BlockSpec((4, gw), lambda i: (0, SSD_INNER // gw)),
        pl.BlockSpec((4, gw), lambda i: (0, SSD_INNER // gw + 1)),
        pl.BlockSpec((1, SSD_INNER), lambda i: (0, 0)),
        pl.BlockSpec((1, gw), lambda i: (0, SSD_INNER // gw)),
        pl.BlockSpec((1, gw), lambda i: (0, SSD_INNER // gw + 1)),
        pl.BlockSpec((1, LANES), lambda i: (0, 0)),
        pl.BlockSpec((1, LANES), lambda i: (0, 0)),
        pl.BlockSpec((1, SSD_INNER), lambda i: (0, 0)),
        pl.BlockSpec((1, SSD_INNER), lambda i: (0, 0)),
        pl.BlockSpec((SEQS_PER_STEP, SSD_GROUPS, SSD_GROUP_DIM, st), lambda i: (i, 0, 0, 0)),
    ]
    state_bytes = SEQS_PER_STEP * SSD_INNER * st * 4
    return pl.pallas_call(
        _ssd_sample_body, grid=(steps,), in_specs=in_specs,
        out_specs=[pl.BlockSpec((n, SSD_INNER), lambda i: (i, 0)),
                   pl.BlockSpec((SEQS_PER_STEP, SSD_GROUPS, SSD_GROUP_DIM, st), lambda i: (i, 0, 0, 0))],
        out_shape=[jax.ShapeDtypeStruct((n_seq * SAMPLE_LEN, SSD_INNER), F32),
                   jax.ShapeDtypeStruct((n_seq, SSD_GROUPS, SSD_GROUP_DIM, st), F32)],
        compiler_params=_vmem_params(4 * state_bytes + (8 << 20), ("arbitrary",)),
        name="ssd_sample",
    )(proj, proj, proj, proj, dt_raw, prev, prev, prev, p["conv_w"], p["conv_w"], p["conv_w"], p["conv_b"],
      p["conv_b"], p["conv_b"], p["dt_bias"], p["a_log"], p["d_lanes"], p["norm"], h0)


def _ml_prompt_body(xc_ref, v_ref, o_ref, gt_ref, gb_ref, wq_ref, wk_ref, cw_ref, cb_ref, ng_ref,
                    y_ref, cs_ref, ns_ref, ms_ref, px_ref):
    c = xc_ref.shape[0]
    k, h = pl.program_id(1), pl.program_id(2)

    @pl.when(k == 0)
    def _init():
        cs_ref[0, h] = jnp.zeros((ML_V_DIM, ML_QK_DIM), F32)
        ns_ref[0, h] = jnp.zeros((1, ML_QK_DIM), F32)
        ms_ref[0, h] = jnp.zeros((1, LANES), F32)
        px_ref[h] = jnp.zeros((SUBLANES, ML_V_DIM), F32)

    u = xc_ref[...]
    xb = _conv_silu(u, px_ref[h], cw_ref[...], cb_ref[...]).astype(BF16)
    px_ref[h] = u[c - SUBLANES:]
    q = jnp.dot(xb, wq_ref[0], preferred_element_type=F32) * (ML_QK_DIM ** -0.5)
    kk = jnp.dot(xb, wk_ref[0], preferred_element_type=F32)
    q_bf, k_bf = q.astype(BF16), kk.astype(BF16)
    v = v_ref[...]

    gates = gt_ref[...] + gb_ref[...]
    lane = lax.broadcasted_iota(I32, gates.shape, 1)
    log_f = jnp.where((lane >= ML_HEADS) & (lane < 2 * ML_HEADS), -_softplus(-gates), 0.0)
    ig = _lane_col(gates, h)
    bcol = _lane_col(_cumsum_rows(log_f), ML_HEADS + h)

    row = lax.broadcasted_iota(I32, (c, c), 0)
    col = lax.broadcasted_iota(I32, (c, c), 1)
    eye, causal = row == col, col <= row
    logd = jnp.where(causal, bcol - _col_to_row(bcol, eye) + _col_to_row(ig, eye), NEG)
    m_prev = ms_ref[0, h][:, 0:1]
    inter = bcol + m_prev
    mt = jnp.maximum(jnp.max(logd, axis=1, keepdims=True), inter)
    s = _dot_t(q_bf, k_bf) * jnp.exp(logd - mt)
    wi = jnp.exp(inter - mt)
    c_old, n_old = cs_ref[0, h], ns_ref[0, h]
    num = jnp.dot(s.astype(BF16), v.astype(BF16), preferred_element_type=F32) + wi * _dot_t(q_bf, c_old.astype(BF16))
    den = jnp.sum(s, axis=1, keepdims=True) + wi * jnp.sum(q * n_old, axis=1, keepdims=True)
    hout = num / jnp.maximum(jnp.abs(den), jnp.exp(-mt))
    hout = hout * lax.rsqrt(jnp.mean(hout * hout, axis=1, keepdims=True) + EPS)
    y_ref[...] = (hout * ng_ref[...] * jax.nn.sigmoid(o_ref[...])).astype(y_ref.dtype)

    blast = bcol[c - 1:c]
    gcol = blast - bcol + ig
    carry = blast + m_prev
    m_new = jnp.maximum(carry, jnp.max(gcol, axis=0, keepdims=True))
    w_s, w_c = jnp.exp(gcol - m_new), jnp.exp(carry - m_new)
    cs_ref[0, h] = w_c * c_old + _tdot((v * w_s).astype(BF16), k_bf)
    ns_ref[0, h] = w_c * n_old + jnp.sum(w_s * kk, axis=0, keepdims=True)
    ms_ref[0, h] = jnp.broadcast_to(m_new, (1, LANES))


def _ml_prompt(proj, gates, p, n_seq, seq_len):
    c = ML_CHUNK
    nc = seq_len // c
    dv, dk = ML_V_DIM, ML_QK_DIM
    in_specs = [
        pl.BlockSpec((c, dv), lambda b, k, h: (b * nc + k, h)),
        pl.BlockSpec((c, dv), lambda b, k, h: (b * nc + k, ML_HEADS + h)),
        pl.BlockSpec((c, dv), lambda b, k, h: (b * nc + k, 2 * ML_HEADS + h)),
        pl.BlockSpec((c, LANES), lambda b, k, h: (b * nc + k, 0)),
        pl.BlockSpec((1, LANES), lambda b, k, h: (0, 0)),
        pl.BlockSpec((1, dv, dk), lambda b, k, h: (h, 0, 0)),
        pl.BlockSpec((1, dv, dk), lambda b, k, h: (h, 0, 0)),
        pl.BlockSpec((4, dv), lambda b, k, h: (0, h)),
        pl.BlockSpec((1, dv), lambda b, k, h: (0, h)),
        pl.BlockSpec((1, dv), lambda b, k, h: (0, h)),
    ]
    return pl.pallas_call(
        _ml_prompt_body, grid=(n_seq, nc, ML_HEADS), in_specs=in_specs,
        out_specs=[pl.BlockSpec((c, dv), lambda b, k, h: (b * nc + k, h)),
                   pl.BlockSpec((1, ML_HEADS, dv, dk), lambda b, k, h: (b, 0, 0, 0)),
                   pl.BlockSpec((1, ML_HEADS, 1, dk), lambda b, k, h: (b, 0, 0, 0)),
                   pl.BlockSpec((1, ML_HEADS, 1, LANES), lambda b, k, h: (b, 0, 0, 0))],
        out_shape=[jax.ShapeDtypeStruct((n_seq * seq_len, ML_INNER), BF16),
                   jax.ShapeDtypeStruct((n_seq, ML_HEADS, dv, dk), F32),
                   jax.ShapeDtypeStruct((n_seq, ML_HEADS, 1, dk), F32),
                   jax.ShapeDtypeStruct((n_seq, ML_HEADS, 1, LANES), F32)],
        scratch_shapes=[pltpu.VMEM((ML_HEADS, SUBLANES, dv), F32)],
        compiler_params=_vmem_params(4 * ML_HEADS * dv * dk * 4 + (8 << 20), ("arbitrary",) * 3),
        name="mlstm_prompt",
    )(proj, proj, proj, gates, p["gate_bias"], p["wq"], p["wk"], p["conv_w"], p["conv_b"], p["norm"])


def _ml_sample_body(xc_ref, v_ref, o_ref, gt_ref, prev_ref, gb_ref, wq_ref, wk_ref, cw_ref, cb_ref, ng_ref,
                    c0_ref, n0_ref, m0_ref, y_ref, cs_ref, ns_ref, ms_ref):
    n = SUBLANES
    row1 = lax.broadcasted_iota(I32, (n, 1), 0)
    pos1, first1 = row1 % SAMPLE_LEN, row1 < SAMPLE_LEN

    def src(v, s):
        return jnp.where(first1, v[s:s + 1], v[SAMPLE_LEN + s:SAMPLE_LEN + s + 1])

    def per_seq(vals):
        return jnp.where(first1, vals[0], vals[1])

    xs = _conv_silu_sample(xc_ref[...], prev_ref[...], cw_ref[...], cb_ref[...], pos1)
    gates = gt_ref[...] + gb_ref[...]
    lane = lax.broadcasted_iota(I32, gates.shape, 1)
    log_f = jnp.where((lane >= ML_HEADS) & (lane < 2 * ML_HEADS), -_softplus(-gates), 0.0)
    bsum = log_f + jnp.where(pos1 >= 1, pltpu.roll(log_f, 1, axis=0), 0.0)
    bsum = bsum + jnp.where(pos1 >= 2, pltpu.roll(bsum, 2, axis=0), 0.0)
    v_all, o_all = v_ref[...], o_ref[...]
    pad_rows = LANES - n

    ys = []
    for h in range(ML_HEADS):
        hs = slice(h * ML_V_DIM, (h + 1) * ML_V_DIM)
        xb = xs[:, hs].astype(BF16)
        q = jnp.dot(xb, wq_ref[h], preferred_element_type=F32) * (ML_QK_DIM ** -0.5)
        kk = jnp.dot(xb, wk_ref[h], preferred_element_type=F32)
        v = v_all[:, hs]
        ig = gates[:, h:h + 1]
        bcol = bsum[:, ML_HEADS + h:ML_HEADS + h + 1]
        c_old = [c0_ref[s_, h] for s_ in range(SEQS_PER_STEP)]
        n_old = [n0_ref[s_, h] for s_ in range(SEQS_PER_STEP)]
        m_old = [m0_ref[s_, h][:, 0:1] for s_ in range(SEQS_PER_STEP)]
        inter = bcol + per_seq(m_old)
        logd = [jnp.where(pos1 >= s, bcol - src(bcol, s) + src(ig, s), NEG) for s in range(SAMPLE_LEN)]
        mt = inter
        for ld in logd:
            mt = jnp.maximum(mt, ld)
        wi = jnp.exp(inter - mt)
        q_bf = q.astype(BF16)
        num = wi * per_seq([_dot_t(q_bf, cq.astype(BF16)) for cq in c_old])
        den = wi * jnp.sum(q * per_seq(n_old), axis=1, keepdims=True)
        for s in range(SAMPLE_LEN):
            sw = jnp.sum(q * src(kk, s), axis=1, keepdims=True) * jnp.exp(logd[s] - mt)
            num = num + sw * src(v, s)
            den = den + sw
        hout = num / jnp.maximum(jnp.abs(den), jnp.exp(-mt))
        hout = hout * lax.rsqrt(jnp.mean(hout * hout, axis=1, keepdims=True) + EPS)
        ys.append(hout * ng_ref[:, hs] * jax.nn.sigmoid(o_all[:, hs]))

        blast = src(bcol, SAMPLE_LEN - 1)
        gcol = blast - bcol + ig
        k_pad = jnp.concatenate([kk, jnp.zeros((pad_rows, ML_QK_DIM), F32)], axis=0).astype(BF16)
        for s_ in range(SEQS_PER_STEP):
            own = (row1 >= s_ * SAMPLE_LEN) & (row1 < (s_ + 1) * SAMPLE_LEN)
            last_row = (s_ + 1) * SAMPLE_LEN - 1
            carry = bcol[last_row:last_row + 1] + m_old[s_]
            m_new = jnp.maximum(carry, jnp.max(jnp.where(own, gcol, NEG), axis=0, keepdims=True))
            w_s = jnp.where(own, jnp.exp(gcol - m_new), 0.0)
            w_c = jnp.exp(carry - m_new)
            vw_pad = jnp.concatenate([v * w_s, jnp.zeros((pad_rows, ML_V_DIM), F32)], axis=0).astype(BF16)
            cs_ref[s_, h] = w_c * c_old[s_] + _tdot(vw_pad, k_pad)
            ns_ref[s_, h] = w_c * n_old[s_] + jnp.sum(w_s * kk, axis=0, keepdims=True)
            ms_ref[s_, h] = jnp.broadcast_to(m_new, (1, LANES))
    y_ref[...] = jnp.concatenate(ys, axis=1)


def _ml_sample(proj, gates, prev, c0, n0, m0, p, row0, n_seq):
    n = SUBLANES
    steps = n_seq // SEQS_PER_STEP
    r0 = row0 // n
    dv, dk = ML_V_DIM, ML_QK_DIM
    sq = SEQS_PER_STEP
    in_specs = [
        pl.BlockSpec((n, ML_INNER), lambda i: (r0 + i, 0)),
        pl.BlockSpec((n, ML_INNER), lambda i: (r0 + i, 1)),
        pl.BlockSpec((n, ML_INNER), lambda i: (r0 + i, 2)),
        pl.BlockSpec((n, LANES), lambda i: (r0 + i, 0)),
        pl.BlockSpec((n, ML_INNER), lambda i: (i, 0)),
        pl.BlockSpec((1, LANES), lambda i: (0, 0)),
        pl.BlockSpec((ML_HEADS, dv, dk), lambda i: (0, 0, 0)),
        pl.BlockSpec((ML_HEADS, dv, dk), lambda i: (0, 0, 0)),
        pl.BlockSpec((4, ML_INNER), lambda i: (0, 0)),
        pl.BlockSpec((1, ML_INNER), lambda i: (0, 0)),
        pl.BlockSpec((1, ML_INNER), lambda i: (0, 0)),
        pl.BlockSpec((sq, ML_HEADS, dv, dk), lambda i: (i, 0, 0, 0)),
        pl.BlockSpec((sq, ML_HEADS, 1, dk), lambda i: (i, 0, 0, 0)),
        pl.BlockSpec((sq, ML_HEADS, 1, LANES), lambda i: (i, 0, 0, 0)),
    ]
    state_bytes = sq * ML_HEADS * dv * dk * 4
    return pl.pallas_call(
        _ml_sample_body, grid=(steps,), in_specs=in_specs,
        out_specs=[pl.BlockSpec((n, ML_INNER), lambda i: (i, 0)),
                   pl.BlockSpec((sq, ML_HEADS, dv, dk), lambda i: (i, 0, 0, 0)),
                   pl.BlockSpec((sq, ML_HEADS, 1, dk), lambda i: (i, 0, 0, 0)),
                   pl.BlockSpec((sq, ML_HEADS, 1, LANES), lambda i: (i, 0, 0, 0))],
        out_shape=[jax.ShapeDtypeStruct((n_seq * SAMPLE_LEN, ML_INNER), F32),
                   jax.ShapeDtypeStruct((n_seq, ML_HEADS, dv, dk), F32),
                   jax.ShapeDtypeStruct((n_seq, ML_HEADS, 1, dk), F32),
                   jax.ShapeDtypeStruct((n_seq, ML_HEADS, 1, LANES), F32)],
        compiler_params=_vmem_params(4 * state_bytes + (12 << 20), ("arbitrary",)),
        name="mlstm_sample",
    )(proj, proj, proj, gates, prev, p["gate_bias"], p["wq"], p["wk"], p["conv_w"], p["conv_b"], p["norm"],
      c0, n0, m0)


def _routing_tables(top_e, gates, n_blocks):
    n_assign = top_e.size
    e_flat = top_e.reshape(-1)
    onehot = (e_flat[:, None] == jnp.arange(N_EXPERTS, dtype=I32)[None, :]).astype(I32)
    csum = jnp.cumsum(onehot, axis=0)
    rank = jnp.sum((csum - onehot) * onehot, axis=1)
    counts = csum[-1]
    padded = (counts + MOE_ROWS - 1) // MOE_ROWS * MOE_ROWS
    pends = jnp.cumsum(padded)
    pos = jnp.sum(onehot * (pends - padded)[None, :], axis=1) + rank
    n_rows = n_blocks * MOE_ROWS
    slot_tok = jnp.zeros((n_rows,), I32).at[pos].set(jnp.arange(n_assign, dtype=I32) // TOP_K)
    slot_gate = jnp.zeros((n_rows,), F32).at[pos].set(gates.reshape(-1))
    block_start = jnp.arange(n_blocks, dtype=I32) * MOE_ROWS
    block_e = jnp.minimum(jnp.searchsorted(pends, block_start, side="right"), N_EXPERTS - 1).astype(I32)
    recast = jnp.concatenate([jnp.ones((1,), I32), (block_e[1:] != block_e[:-1]).astype(I32)])
    n_used = (pends[-1:] // MOE_ROWS).astype(I32)
    return pos, slot_tok, slot_gate, block_e, recast, n_used


def kernel(x_prompt, x_sample, state_ssm, state_ssm_conv, state_mlstm_c, state_mlstm_n, state_mlstm_m, state_mlstm_conv, norm_mix_a, ssd_w_in, ssd_conv_w, ssd_conv_b, ssd_dt_bias, ssd_a_log, ssd_d, ssd_norm, ssd_w_out, norm_ffn_a, ffn_w_gate, ffn_w_up, ffn_w_down, norm_mix_b, ml_w_in, ml_conv_w, ml_conv_b, ml_w_q, ml_w_k, ml_b_i, ml_b_f, ml_norm, ml_w_out, norm_ffn_b, moe_w_router, moe_b_router, moe_w_gate, moe_w_up, moe_w_down, final_norm):
    bp, lp, d = x_prompt.shape
    bs, ls, _ = x_sample.shape
    assert ls == SAMPLE_LEN and lp % ML_CHUNK == 0 and lp % SSD_CHUNK == 0 and bs % SEQS_PER_STEP == 0
    tp, ts = bp * lp, bs * ls
    t = tp + ts
    assert tp % ROW_TILE == 0 and ts % ROW_TILE == 0
    x = jnp.concatenate([x_prompt.reshape(tp, d), x_sample.reshape(ts, d)], axis=0)

    def pad_lanes(v):
        return jnp.pad(v.reshape(1, -1), ((0, 0), (0, LANES - v.size)))

    def prev_rows(state):
        return jnp.pad(state, ((0, 0), (1, 0), (0, 0))).reshape(state.shape[0] * SAMPLE_LEN, state.shape[2])

    z_cols = SSD_INNER + SSD_XBC
    xn = _rmsnorm(x, norm_mix_a[0], BF16)
    proj = _matmul(xn, [ssd_w_in[0][None]], tm=ROW_TILE, tn=1024, n_cols=z_cols, out_dtype=F32, name="ssd_in")
    dt_raw = _narrow_matmul(xn, ssd_w_in[0][:, z_cols:], "ssd_dt")
    ssd_p = dict(conv_w=ssd_conv_w[0], conv_b=ssd_conv_b[0].reshape(1, -1), dt_bias=pad_lanes(ssd_dt_bias[0]),
                 a_log=pad_lanes(ssd_a_log[0]), d_lanes=jnp.repeat(ssd_d[0], SSD_HEAD_DIM).reshape(1, -1),
                 norm=ssd_norm[0].reshape(1, -1))
    y_p, ssm_p = _ssd_prompt(proj, dt_raw, ssd_p, bp, lp)
    y_s, ssm_s = _ssd_sample(proj, dt_raw, prev_rows(state_ssm_conv[0]),
                             state_ssm[0].reshape(bs, SSD_GROUPS, SSD_GROUP_DIM, SSD_STATE), ssd_p, tp, bs)
    ssm_conv_p = proj[:tp].reshape(bp, lp, z_cols)[:, lp - 3:, SSD_INNER:]
    ssm_conv_s = proj[tp:].reshape(bs, ls, z_cols)[:, ls - 3:, SSD_INNER:]
    y = jnp.concatenate([y_p, y_s.astype(BF16)], axis=0)
    x = _matmul(y, [ssd_w_out[0][None]], tm=ROW_TILE, tn=512, n_cols=d, out_dtype=F32, name="ssd_out", res=x)

    xn = _rmsnorm(x, norm_ffn_a[0], BF16)
    hid = _matmul(xn, [ffn_w_gate[0][None], ffn_w_up[0][None]], tm=ROW_TILE, tn=512, n_cols=ffn_w_gate.shape[2],
                  out_dtype=BF16, name="ffn_gate_up")
    x = _matmul(hid, [ffn_w_down[0][None]], tm=ROW_TILE, tn=512, n_cols=d, out_dtype=F32, name="ffn_down", res=x)

    xn = _rmsnorm(x, norm_mix_b[0], BF16)
    ml_cols = 3 * ML_INNER
    proj2 = _matmul(xn, [ml_w_in[0][None]], tm=ROW_TILE, tn=1024, n_cols=ml_cols, out_dtype=F32, name="ml_in")
    gates = _narrow_matmul(xn, ml_w_in[0][:, ml_cols:], "ml_gates")
    ml_p = dict(conv_w=ml_conv_w[0], conv_b=ml_conv_b[0].reshape(1, -1),
                gate_bias=pad_lanes(jnp.concatenate([ml_b_i[0], ml_b_f[0]])),
                wq=ml_w_q[0].astype(BF16), wk=ml_w_k[0].astype(BF16), norm=ml_norm[0].reshape(1, -1))
    y_p, c_p, n_p, m_p = _ml_prompt(proj2, gates, ml_p, bp, lp)
    y_s, c_s, n_s, m_s = _ml_sample(
        proj2, gates, prev_rows(state_mlstm_conv[0]), state_mlstm_c[0],
        state_mlstm_n[0].reshape(bs, ML_HEADS, 1, ML_QK_DIM),
        jnp.broadcast_to(state_mlstm_m[0][:, :, None, None], (bs, ML_HEADS, 1, LANES)), ml_p, tp, bs)
    ml_conv_p = proj2[:tp].reshape(bp, lp, ml_cols)[:, lp - 3:, :ML_INNER]
    ml_conv_s = proj2[tp:].reshape(bs, ls, ml_cols)[:, ls - 3:, :ML_INNER]
    y = jnp.concatenate([y_p, y_s.astype(BF16)], axis=0)
    x = _matmul(y, [ml_w_out[0][None]], tm=ROW_TILE, tn=512, n_cols=d, out_dtype=F32, name="ml_out", res=x)

    xn = _rmsnorm(x, norm_ffn_b[0], BF16)
    top_e, top_g = _router(xn, moe_w_router[0], moe_b_router[0])
    n_blocks = -(-(t * TOP_K + N_EXPERTS * (MOE_ROWS - 1)) // MOE_ROWS)
    pos, slot_tok, slot_gate, block_e, recast, n_used = _routing_tables(top_e, top_g, n_blocks)
    xg = xn[slot_tok]
    route = dict(block_e=block_e, recast=recast, n_used=n_used)
    hid = _matmul(xg, [moe_w_gate[0], moe_w_up[0]], tm=MOE_ROWS, tn=512, n_cols=moe_w_gate.shape[3],
                  out_dtype=BF16, name="moe_gate_up", **route)
    yb = _matmul(hid, [moe_w_down[0]], tm=MOE_ROWS, tn=512, n_cols=d, out_dtype=F32, name="moe_down",
                 scale=slot_gate.reshape(-1, 1), **route)
    x = x + yb[pos].reshape(t, TOP_K, d).sum(axis=1)

    out = _rmsnorm(x, final_norm, F32)
    return (out[:tp].reshape(bp, lp, d), out[tp:].reshape(bs, ls, d),
            ssm_p.reshape(1, bp, *state_ssm.shape[2:]), ssm_conv_p[None],
            c_p[None], n_p.reshape(1, bp, ML_HEADS, ML_QK_DIM), m_p[:, :, 0, 0][None], ml_conv_p[None],
            ssm_s.reshape(1, bs, *state_ssm.shape[2:]), ssm_conv_s[None],
            c_s[None], n_s.reshape(1, bs, ML_HEADS, ML_QK_DIM), m_s[:, :, 0, 0][None], ml_conv_s[None])
```

```python
import functools

import jax
import jax.numpy as jnp
from jax import lax
from jax.experimental import pallas as pl
from jax.experimental.pallas import tpu as pltpu

F32, BF16, I32 = jnp.float32, jnp.bfloat16, jnp.int32

EPS = 1e-6
D_MODEL = 2048
SSD_INNER = 4096
SSD_HEAD_DIM = 64
SSD_GROUPS = 8
SSD_STATE = 128
SSD_GROUP_DIM = SSD_INNER // SSD_GROUPS
SSD_HEADS_PER_GROUP = SSD_GROUP_DIM // SSD_HEAD_DIM
SSD_XBC = SSD_INNER + 2 * SSD_GROUPS * SSD_STATE
ML_HEADS = 8
ML_INNER = 4096
ML_V_DIM = 512
ML_QK_DIM = 256
N_EXPERTS = 8
TOP_K = 2

LANES = 128
SUBLANES = 8
SSD_CHUNK = 128
ML_CHUNK = 256
SAMPLE_LEN = 4
SEQS_PER_STEP = SUBLANES // SAMPLE_LEN
ROW_TILE = 512
MOE_ROWS = 512
COMBINE_ROWS = 256
NEG = -1e30
VMEM_CAP = 60 << 20


def _vmem_params(nbytes, semantics):
    return pltpu.CompilerParams(dimension_semantics=semantics,
                                vmem_limit_bytes=int(min(nbytes + (8 << 20), VMEM_CAP)))


def _softplus(x):
    return jnp.maximum(x, 0.0) + jnp.log1p(jnp.exp(-jnp.abs(x)))


def _silu(x):
    return x * jax.nn.sigmoid(x)


def _cumsum_rows(x):
    n = x.shape[0]
    row = lax.broadcasted_iota(I32, x.shape, 0)
    s = 1
    while s < n:
        x = x + jnp.where(row >= s, pltpu.roll(x, s, axis=0), 0.0)
        s *= 2
    return x


def _lane_col(x, lane_idx):
    lane = lax.broadcasted_iota(I32, x.shape, 1)
    return jnp.sum(jnp.where(lane == lane_idx, x, 0.0), axis=1, keepdims=True)


def _col_to_row(col, eye):
    return jnp.sum(jnp.where(eye, col, 0.0), axis=0, keepdims=True)


def _conv_silu(u, prev, w, b):
    c, width = u.shape
    row8 = lax.broadcasted_iota(I32, (SUBLANES, width), 0)
    acc = b + u * w[3:4]
    for d in (1, 2, 3):
        sh = pltpu.roll(u, d, axis=0)
        head = jnp.where(row8 < d, pltpu.roll(prev, d, axis=0), sh[:SUBLANES])
        sh = head if c == SUBLANES else jnp.concatenate([head, sh[SUBLANES:]], axis=0)
        acc = acc + sh * w[3 - d:4 - d]
    return _silu(acc)


def _conv_silu_sample(u, prev, w, b, pos):
    acc = b + u * w[3:4]
    for d in (1, 2, 3):
        sh = jnp.where(pos >= d, pltpu.roll(u, d, axis=0), pltpu.roll(prev, d + SAMPLE_LEN, axis=0))
        acc = acc + sh * w[3 - d:4 - d]
    return _silu(acc)


def _dot_t(a, b):
    return lax.dot_general(a, b, (((1,), (1,)), ((), ())), preferred_element_type=F32)


def _tdot(a, b):
    return lax.dot_general(a, b, (((0,), (0,)), ((), ())), preferred_element_type=F32)


def _rms(x, g):
    return x * lax.rsqrt(jnp.mean(x * x, axis=-1, keepdims=True) + EPS) * g


def _rmsnorm_body(x_ref, g_ref, o_ref):
    o_ref[...] = _rms(x_ref[...], g_ref[...]).astype(o_ref.dtype)


def _rmsnorm(x, g, out_dtype):
    t, d = x.shape
    row_spec = pl.BlockSpec((ROW_TILE, d), lambda i: (i, 0))
    return pl.pallas_call(
        _rmsnorm_body, grid=(t // ROW_TILE,),
        in_specs=[row_spec, pl.BlockSpec((1, d), lambda i: (0, 0))],
        out_specs=row_spec, out_shape=jax.ShapeDtypeStruct((t, d), out_dtype), name="rmsnorm",
    )(x, g.reshape(1, d))


def _matmul_body(be_ref, rc_ref, nu_ref, a_ref, *rest, n_w, tail_from, has_scale, has_res):
    del be_ref
    tail_ref = rest[0] if tail_from is not None else None
    rest = rest[1:] if tail_from is not None else rest
    w_refs, rest = rest[:n_w], rest[n_w:]
    scale_ref = rest[0] if has_scale else None
    rest = rest[1:] if has_scale else rest
    res_ref = rest[0] if has_res else None
    rest = rest[1:] if has_res else rest
    o_ref, wb_refs = rest[0], rest[1:]
    b = pl.program_id(1)

    @pl.when(rc_ref[b] == 1)
    def _recast():
        for w_ref, wb_ref in zip(w_refs, wb_refs):
            wb_ref[...] = w_ref[...].astype(BF16)

    def compute(a):
        acc = jnp.dot(a, wb_refs[0][...], preferred_element_type=F32)
        if n_w == 2:
            acc = _silu(acc) * jnp.dot(a, wb_refs[1][...], preferred_element_type=F32)
        if has_scale:
            acc = acc * scale_ref[...]
        if has_res:
            acc = acc + res_ref[...]
        o_ref[...] = acc.astype(o_ref.dtype)

    if tail_from is None:
        pl.when(b < nu_ref[0])(lambda: compute(a_ref[...]))
    else:
        pl.when((b < nu_ref[0]) & (b < tail_from))(lambda: compute(a_ref[...]))
        pl.when((b < nu_ref[0]) & (b >= tail_from))(lambda: compute(tail_ref[...]))

    @pl.when(b >= nu_ref[0])
    def _unused():
        o_ref[...] = jnp.zeros(o_ref.shape, o_ref.dtype)


def _matmul(a, ws, *, tm, tn, n_cols, out_dtype, name, a_tail=None, block_e=None, recast=None, n_used=None,
            scale=None, res=None):
    k = a.shape[1]
    m = a.shape[0] + (0 if a_tail is None else a_tail.shape[0])
    nb, nj, n_w = m // tm, n_cols // tn, len(ws)
    tail_from = None if a_tail is None else a.shape[0] // tm
    if block_e is None:
        block_e = jnp.zeros((nb,), I32)
        recast = jnp.zeros((nb,), I32).at[0].set(1)
        n_used = jnp.full((1,), nb, I32)
    if a_tail is None:
        in_specs, args = [pl.BlockSpec((tm, k), lambda j, b, be, rc, nu: (b, 0))], [a]
    else:
        assert a.shape[0] % tm == 0 and a_tail.shape[0] % tm == 0
        in_specs = [pl.BlockSpec((tm, k), lambda j, b, be, rc, nu: (jnp.minimum(b, tail_from - 1), 0)),
                    pl.BlockSpec((tm, k), lambda j, b, be, rc, nu: (jnp.maximum(b - tail_from, 0), 0))]
        args = [a, a_tail]
    in_specs += [pl.BlockSpec((None, k, tn), lambda j, b, be, rc, nu: (be[b], 0, j))] * n_w
    args += list(ws)
    if scale is not None:
        in_specs.append(pl.BlockSpec((tm, 1), lambda j, b, be, rc, nu: (b, 0)))
        args.append(scale)
    if res is not None:
        in_specs.append(pl.BlockSpec((tm, tn), lambda j, b, be, rc, nu: (b, j)))
        args.append(res)
    out_bytes = jnp.dtype(out_dtype).itemsize
    vmem = (n_w * k * tn * (2 * 4 + 2) + (2 if a_tail is None else 4) * tm * k * 2 + 2 * tm * tn * out_bytes
            + (2 * tm * tn * 4 if res is not None else 0) + 2 * tm * tn * 4)
    grid_spec = pltpu.PrefetchScalarGridSpec(
        num_scalar_prefetch=3, grid=(nj, nb), in_specs=in_specs,
        out_specs=pl.BlockSpec((tm, tn), lambda j, b, be, rc, nu: (b, j)),
        scratch_shapes=[pltpu.VMEM((k, tn), BF16)] * n_w)
    body = functools.partial(_matmul_body, n_w=n_w, tail_from=tail_from, has_scale=scale is not None,
                             has_res=res is not None)
    return pl.pallas_call(
        body, grid_spec=grid_spec, out_shape=jax.ShapeDtypeStruct((m, n_cols), out_dtype), name=name,
        compiler_params=_vmem_params(vmem, ("arbitrary", "arbitrary")),
    )(block_e, recast, n_used, *args)


def _narrow_body(a_ref, w_ref, o_ref, *, n_valid):
    lane = lax.broadcasted_iota(I32, w_ref.shape, 1)
    w = jnp.where(lane < n_valid, w_ref[...], 0.0)
    o_ref[...] = jnp.dot(a_ref[...], w.astype(BF16), preferred_element_type=F32)


def _narrow_matmul(a, w, col0, name):
    t, k = a.shape
    n_valid = w.shape[2] - col0
    assert col0 % LANES == 0 and 0 < n_valid <= LANES
    return pl.pallas_call(
        functools.partial(_narrow_body, n_valid=n_valid), grid=(t // ROW_TILE,),
        in_specs=[pl.BlockSpec((ROW_TILE, k), lambda i: (i, 0)),
                  pl.BlockSpec((None, k, LANES), lambda i: (0, 0, col0 // LANES))],
        out_specs=pl.BlockSpec((ROW_TILE, LANES), lambda i: (i, 0)),
        out_shape=jax.ShapeDtypeStruct((t, LANES), F32), name=name,
    )(a, w)


def _router_body(x_ref, ng_ref, w_ref, b_ref, e_ref, g_ref):
    xn = _rms(x_ref[...], ng_ref[...]).astype(BF16)
    logits = jnp.dot(xn, w_ref[...].astype(BF16), preferred_element_type=F32) + b_ref[...]
    lane = lax.broadcasted_iota(I32, logits.shape, 1)
    lane_f = lane.astype(F32)
    m1 = jnp.max(logits, axis=1, keepdims=True)
    i1 = jnp.min(jnp.where(logits == m1, lane_f, float(LANES)), axis=1, keepdims=True)
    rest = jnp.where(lane_f == i1, 2 * NEG, logits)
    m2 = jnp.max(rest, axis=1, keepdims=True)
    i2 = jnp.min(jnp.where(rest == m2, lane_f, float(LANES)), axis=1, keepdims=True)
    e2 = jnp.exp(m2 - m1)
    den = 1.0 + e2
    e_ref[...] = jnp.where(lane == 0, i1, jnp.where(lane == 1, i2, 0.0)).astype(I32)
    g_ref[...] = jnp.where(lane == 0, 1.0 / den, jnp.where(lane == 1, e2 / den, 0.0))


def _router(x, norm_g, w, b):
    t, k = x.shape
    wp = jnp.pad(w, ((0, 0), (0, LANES - N_EXPERTS)))
    bp = jnp.pad(b.reshape(1, N_EXPERTS), ((0, 0), (0, LANES - N_EXPERTS)), constant_values=NEG)
    e, g = pl.pallas_call(
        _router_body, grid=(t // ROW_TILE,),
        in_specs=[pl.BlockSpec((ROW_TILE, k), lambda i: (i, 0)), pl.BlockSpec((1, k), lambda i: (0, 0)),
                  pl.BlockSpec((k, LANES), lambda i: (0, 0)), pl.BlockSpec((1, LANES), lambda i: (0, 0))],
        out_specs=[pl.BlockSpec((ROW_TILE, LANES), lambda i: (i, 0))] * 2,
        out_shape=[jax.ShapeDtypeStruct((t, LANES), I32), jax.ShapeDtypeStruct((t, LANES), F32)],
        name="moe_router",
    )(x, norm_g.reshape(1, k), wp, bp)
    return e[:, :TOP_K], g[:, :TOP_K]


def _gather_body(tok_ref, src_ref, ng_ref, o_ref, buf_ref, sem_ref):
    b, nb = pl.program_id(0), pl.num_programs(0)
    rows = buf_ref.shape[1]

    def row_copy(blk, slot, r):
        return pltpu.make_async_copy(src_ref.at[pl.ds(tok_ref[blk * rows + r], 1)],
                                     buf_ref.at[slot, pl.ds(r, 1)], sem_ref.at[slot])

    def start_block(blk, slot):
        def body(r, carry):
            row_copy(blk, slot, r).start()
            return carry
        lax.fori_loop(0, rows, body, 0, unroll=8)

    @pl.when(b == 0)
    def _first():
        start_block(0, 0)

    @pl.when(b + 1 < nb)
    def _next():
        start_block(b + 1, (b + 1) % 2)

    slot = b % 2

    def wait_body(r, carry):
        row_copy(b, slot, r).wait()
        return carry
    lax.fori_loop(0, rows, wait_body, 0, unroll=8)
    o_ref[...] = _rms(buf_ref[slot], ng_ref[...]).astype(o_ref.dtype)


def _gather_norm_rows(x, norm_g, slot_tok, rows):
    n_rows, d = slot_tok.shape[0], x.shape[1]
    grid_spec = pltpu.PrefetchScalarGridSpec(
        num_scalar_prefetch=1, grid=(n_rows // rows,),
        in_specs=[pl.BlockSpec(memory_space=pl.ANY), pl.BlockSpec((1, d), lambda b, tok: (0, 0))],
        out_specs=pl.BlockSpec((rows, d), lambda b, tok: (b, 0)),
        scratch_shapes=[pltpu.VMEM((2, rows, d), F32), pltpu.SemaphoreType.DMA((2,))])
    return pl.pallas_call(
        _gather_body, grid_spec=grid_spec, out_shape=jax.ShapeDtypeStruct((n_rows, d), BF16),
        compiler_params=_vmem_params(2 * rows * d * 4 + 4 * rows * d * 4, ("arbitrary",)), name="moe_gather",
    )(slot_tok, x, norm_g.reshape(1, d))


def _combine_body(pos_ref, yb_ref, x_ref, g_ref, op_ref, os_ref, buf_ref, sem_ref, *, prompt_tiles):
    i, n = pl.program_id(0), pl.num_programs(0)
    tt = buf_ref.shape[2]

    def row_copy(tile, slot, tl, k):
        return pltpu.make_async_copy(yb_ref.at[pl.ds(pos_ref[(tile * tt + tl) * TOP_K + k], 1)],
                                     buf_ref.at[slot, k, pl.ds(tl, 1)], sem_ref.at[slot])

    def start_tile(tile, slot):
        def body(tl, carry):
            for k in range(TOP_K):
                row_copy(tile, slot, tl, k).start()
            return carry
        lax.fori_loop(0, tt, body, 0, unroll=4)

    @pl.when(i == 0)
    def _first():
        start_tile(0, 0)

    @pl.when(i + 1 < n)
    def _next():
        start_tile(i + 1, (i + 1) % 2)

    slot = i % 2

    def wait_body(tl, carry):
        for k in range(TOP_K):
            row_copy(i, slot, tl, k).wait()
        return carry
    lax.fori_loop(0, tt, wait_body, 0, unroll=4)

    x = x_ref[...] + (buf_ref[slot, 0] + buf_ref[slot, 1])
    y = x * lax.rsqrt(jnp.mean(x * x, axis=-1, keepdims=True) + EPS) * g_ref[...]

    @pl.when(i < prompt_tiles)
    def _prompt():
        op_ref[...] = y

    @pl.when(i >= prompt_tiles)
    def _sample():
        os_ref[...] = y


def _combine(yb, pos, x, g, n_prompt):
    t, d = x.shape
    tt = COMBINE_ROWS
    pt = n_prompt // tt
    grid_spec = pltpu.PrefetchScalarGridSpec(
        num_scalar_prefetch=1, grid=(t // tt,),
        in_specs=[pl.BlockSpec(memory_space=pl.ANY), pl.BlockSpec((tt, d), lambda i, pos: (i, 0)),
                  pl.BlockSpec((1, d), lambda i, pos: (0, 0))],
        out_specs=[pl.BlockSpec((tt, d), lambda i, pos: (jnp.minimum(i, pt - 1), 0)),
                   pl.BlockSpec((tt, d), lambda i, pos: (jnp.maximum(i - pt, 0), 0))],
        scratch_shapes=[pltpu.VMEM((2, TOP_K, tt, d), F32), pltpu.SemaphoreType.DMA((2,))])
    return pl.pallas_call(
        functools.partial(_combine_body, prompt_tiles=pt), grid_spec=grid_spec,
        out_shape=[jax.ShapeDtypeStruct((n_prompt, d), F32), jax.ShapeDtypeStruct((t - n_prompt, d), F32)],
        compiler_params=_vmem_params((2 * TOP_K + 6) * tt * d * 4, ("arbitrary",)), name="moe_combine",
    )(pos, yb, x, g.reshape(1, d))


def _ssd_prompt_body(z_ref, xr_ref, br_ref, cr_ref, dt_ref, cwx_ref, cwb_ref, cwc_ref, cbx_ref, cbb_ref,
                     cbc_ref, dtb_ref, alog_ref, dsk_ref, ng_ref, y_ref, h_ref, px_ref, pb_ref, pc_ref):
    c = xr_ref.shape[0]
    k, g = pl.program_id(1), pl.program_id(2)

    @pl.when(k == 0)
    def _init():
        h_ref[0, g] = jnp.zeros((SSD_GROUP_DIM, SSD_STATE), F32)
        px_ref[g] = jnp.zeros((SUBLANES, SSD_GROUP_DIM), F32)
        pb_ref[g] = jnp.zeros((SUBLANES, SSD_STATE), F32)
        pc_ref[g] = jnp.zeros((SUBLANES, SSD_STATE), F32)

    xr, br, cr = xr_ref[...], br_ref[...], cr_ref[...]
    xs = _conv_silu(xr, px_ref[g], cwx_ref[...], cbx_ref[...])
    bm = _conv_silu(br, pb_ref[g], cwb_ref[...], cbb_ref[...])
    cm = _conv_silu(cr, pc_ref[g], cwc_ref[...], cbc_ref[...])
    px_ref[g] = xr[c - SUBLANES:]
    pb_ref[g] = br[c - SUBLANES:]
    pc_ref[g] = cr[c - SUBLANES:]

    dt = _softplus(dt_ref[...] + dtb_ref[...])
    cum = _cumsum_rows(dt * (-jnp.exp(alog_ref[...])))

    row = lax.broadcasted_iota(I32, (c, c), 0)
    col = lax.broadcasted_iota(I32, (c, c), 1)
    eye, causal = row == col, col <= row
    lane = lax.broadcasted_iota(I32, (c, LANES), 1)
    low = lane < SSD_HEAD_DIM
    hrow = lax.broadcasted_iota(I32, (LANES, SSD_STATE), 0) < SSD_HEAD_DIM
    bm_bf, cm_bf = bm.astype(BF16), cm.astype(BF16)
    cb = _dot_t(cm_bf, bm_bf)

    ys = []
    for p in range(SSD_HEADS_PER_GROUP // 2):
        ws, ecols, tcols, elast = [], [], [], []
        for j in (2 * p, 2 * p + 1):
            head = g * SSD_HEADS_PER_GROUP + j
            ccol = _lane_col(cum, head)
            dcol = _lane_col(dt, head)
            crow, drow = _col_to_row(ccol, eye), _col_to_row(dcol, eye)
            w = jnp.exp(jnp.where(causal, ccol - crow, NEG)) * cb * drow
            ws.append(w.astype(BF16))
            last = ccol[c - 1:c]
            ecols.append(jnp.exp(ccol))
            tcols.append(jnp.exp(last - ccol) * dcol)
            elast.append(jnp.exp(last))
        xp = xs[:, p * LANES:(p + 1) * LANES]
        rhs = jnp.concatenate([jnp.where(low, xp, 0.0), jnp.where(low, 0.0, xp)], axis=0).astype(BF16)
        y_intra = jnp.dot(jnp.concatenate(ws, axis=1), rhs, preferred_element_type=F32)
        hp = h_ref[0, g, p * LANES:(p + 1) * LANES, :]
        y_inter = _dot_t(cm_bf, hp.astype(BF16)) * jnp.where(low, ecols[0], ecols[1])
        ys.append(y_intra + y_inter)
        xw = (xp * jnp.where(low, tcols[0], tcols[1])).astype(BF16)
        h_ref[0, g, p * LANES:(p + 1) * LANES, :] = (
            jnp.where(hrow, elast[0], elast[1]) * hp + _tdot(xw, bm_bf))

    y = jnp.concatenate(ys, axis=1) + dsk_ref[...] * xs
    y = y * _silu(z_ref[...])
    y = y * lax.rsqrt(jnp.mean(y * y, axis=1, keepdims=True) + EPS)
    y_ref[...] = (y * ng_ref[...]).astype(y_ref.dtype)


def _ssd_prompt(proj, dt_raw, p, n_seq, seq_len):
    c = SSD_CHUNK
    nc = seq_len // c
    gd, st = SSD_GROUP_DIM, SSD_STATE
    xo, bo, co = SSD_INNER // gd, 2 * SSD_INNER // st, (2 * SSD_INNER + SSD_GROUPS * st) // st
    cxo, cbo, cco = 0, SSD_INNER // st, (SSD_INNER + SSD_GROUPS * st) // st

    def rows(f):
        return lambda b, k, g: (b * nc + k, f(g))

    in_specs = [
        pl.BlockSpec((c, gd), rows(lambda g: g)),
        pl.BlockSpec((c, gd), rows(lambda g: xo + g)),
        pl.BlockSpec((c, st), rows(lambda g: bo + g)),
        pl.BlockSpec((c, st), rows(lambda g: co + g)),
        pl.BlockSpec((c, LANES), rows(lambda g: 0)),
        pl.BlockSpec((4, gd), lambda b, k, g: (0, cxo + g)),
        pl.BlockSpec((4, st), lambda b, k, g: (0, cbo + g)),
        pl.BlockSpec((4, st), lambda b, k, g: (0, cco + g)),
        pl.BlockSpec((1, gd), lambda b, k, g: (0, cxo + g)),
        pl.BlockSpec((1, st), lambda b, k, g: (0, cbo + g)),
        pl.BlockSpec((1, st), lambda b, k, g: (0, cco + g)),
        pl.BlockSpec((1, LANES), lambda b, k, g: (0, 0)),
        pl.BlockSpec((1, LANES), lambda b, k, g: (0, 0)),
        pl.BlockSpec((1, gd), lambda b, k, g: (0, g)),
        pl.BlockSpec((1, gd), lambda b, k, g: (0, g)),
    ]
    return pl.pallas_call(
        _ssd_prompt_body, grid=(n_seq, nc, SSD_GROUPS), in_specs=in_specs,
        out_specs=[pl.BlockSpec((c, gd), lambda b, k, g: (b * nc + k, g)),
                   pl.BlockSpec((1, SSD_GROUPS, gd, st), lambda b, k, g: (b, 0, 0, 0))],
        out_shape=[jax.ShapeDtypeStruct((n_seq * seq_len, SSD_INNER), BF16),
                   jax.ShapeDtypeStruct((n_seq, SSD_GROUPS, gd, st), F32)],
        scratch_shapes=[pltpu.VMEM((SSD_GROUPS, SUBLANES, gd), F32), pltpu.VMEM((SSD_GROUPS, SUBLANES, st), F32),
                        pltpu.VMEM((SSD_GROUPS, SUBLANES, st), F32)],
        compiler_params=pltpu.CompilerParams(dimension_semantics=("arbitrary",) * 3),
        name="ssd_prompt",
    )(proj, proj, proj, proj, dt_raw, p["conv_w"], p["conv_w"], p["conv_w"], p["conv_b"], p["conv_b"],
      p["conv_b"], p["dt_bias"], p["a_log"], p["d_lanes"], p["norm"])


def _expand_heads(v, first_head, n_heads):
    lane = lax.broadcasted_iota(I32, (v.shape[0], LANES), 1)
    low = lane < SSD_HEAD_DIM
    out = []
    for q in range(n_heads // 2):
        j = first_head + 2 * q
        out.append(jnp.where(low, v[:, j:j + 1], v[:, j + 1:j + 2]))
    return jnp.concatenate(out, axis=1)


def _ssd_sample_body(z_ref, xr_ref, br_ref, cr_ref, dt_ref, prevx_ref, prevb_ref, prevc_ref, cwx_ref, cwb_ref,
                     cwc_ref, cbx_ref, cbb_ref, cbc_ref, dtb_ref, alog_ref, dsk_ref, ng_ref, h0_ref,
                     y_ref, h_ref):
    n = SUBLANES
    row1 = lax.broadcasted_iota(I32, (n, 1), 0)
    pos1, first1 = row1 % SAMPLE_LEN, row1 < SAMPLE_LEN

    def src(v, s):
        return jnp.where(first1, v[s:s + 1], v[SAMPLE_LEN + s:SAMPLE_LEN + s + 1])

    xs = _conv_silu_sample(xr_ref[...], prevx_ref[...], cwx_ref[...], cbx_ref[...], pos1)
    bm = _conv_silu_sample(br_ref[...], prevb_ref[...], cwb_ref[...], cbb_ref[...], pos1)
    cm = _conv_silu_sample(cr_ref[...], prevc_ref[...], cwc_ref[...], cbc_ref[...], pos1)
    dt = _softplus(dt_ref[...] + dtb_ref[...])
    la = dt * (-jnp.exp(alog_ref[...]))
    cum = la + jnp.where(pos1 >= 1, pltpu.roll(la, 1, axis=0), 0.0)
    cum = cum + jnp.where(pos1 >= 2, pltpu.roll(cum, 2, axis=0), 0.0)
    z = z_ref[...]
    pad_rows = LANES - n
    hrow = lax.broadcasted_iota(I32, (SSD_GROUP_DIM, SSD_STATE), 0) // SSD_HEAD_DIM

    ys = []
    for g in range(SSD_GROUPS):
        gs = slice(g * SSD_GROUP_DIM, (g + 1) * SSD_GROUP_DIM)
        ns = slice(g * SSD_STATE, (g + 1) * SSD_STATE)
        x_g, b_g, c_g = xs[:, gs], bm[:, ns], cm[:, ns]
        cum_e = _expand_heads(cum, g * SSD_HEADS_PER_GROUP, SSD_HEADS_PER_GROUP)
        dt_e = _expand_heads(dt, g * SSD_HEADS_PER_GROUP, SSD_HEADS_PER_GROUP)
        last_e = src(cum_e, SAMPLE_LEN - 1)
        y = jnp.zeros((n, SSD_GROUP_DIM), F32)
        for s in range(SAMPLE_LEN):
            cb_s = jnp.sum(c_g * src(b_g, s), axis=1, keepdims=True)
            decay = jnp.exp(jnp.where(pos1 >= s, cum_e - src(cum_e, s), NEG))
            y = y + decay * cb_s * src(dt_e, s) * src(x_g, s)
        c_bf = c_g.astype(BF16)
        h_old = [h0_ref[q, g] for q in range(SEQS_PER_STEP)]
        y_inter = jnp.where(first1, _dot_t(c_bf, h_old[0].astype(BF16)), _dot_t(c_bf, h_old[1].astype(BF16)))
        y = y + y_inter * jnp.exp(cum_e) + dsk_ref[:, gs] * x_g
        y = y * _silu(z[:, gs])
        y = y * lax.rsqrt(jnp.mean(y * y, axis=1, keepdims=True) + EPS)
        ys.append(y * ng_ref[:, gs])
        xw = x_g * jnp.exp(last_e - cum_e) * dt_e
        b_pad = jnp.concatenate([b_g, jnp.zeros((pad_rows, SSD_STATE), F32)], axis=0).astype(BF16)
        for q in range(SEQS_PER_STEP):
            own = (row1 >= q * SAMPLE_LEN) & (row1 < (q + 1) * SAMPLE_LEN)
            xw_pad = jnp.concatenate([jnp.where(own, xw, 0.0), jnp.zeros((pad_rows, SSD_GROUP_DIM), F32)],
                                     axis=0).astype(BF16)
            last_row = (q + 1) * SAMPLE_LEN - 1
            elast = jnp.exp(cum[last_row:last_row + 1])
            dec = jnp.zeros((SSD_GROUP_DIM, SSD_STATE), F32)
            for j in range(SSD_HEADS_PER_GROUP):
                hd = g * SSD_HEADS_PER_GROUP + j
                dec = jnp.where(hrow == j, elast[:, hd:hd + 1], dec)
            h_ref[q, g] = dec * h_old[q] + _tdot(xw_pad, b_pad)
    y_ref[...] = jnp.concatenate(ys, axis=1).astype(y_ref.dtype)


def _ssd_sample(proj, dt_raw, prev, h0, p, row0, n_seq):
    n = SUBLANES
    steps = n_seq // SEQS_PER_STEP
    r0 = row0 // n
    st = SSD_STATE
    zo, xo, bo, co = 0, 1, 2 * SSD_INNER // (SSD_GROUPS * st), (2 * SSD_INNER + SSD_GROUPS * st) // (SSD_GROUPS * st)
    gw = SSD_GROUPS * st
    in_specs = [
        pl.BlockSpec((n, SSD_INNER), lambda i: (r0 + i, zo)),
        pl.BlockSpec((n, SSD_INNER), lambda i: (r0 + i, xo)),
        pl.BlockSpec((n, gw), lambda i: (r0 + i, bo)),
        pl.BlockSpec((n, gw), lambda i: (r0 + i, co)),
        pl.BlockSpec((n, LANES), lambda i: (r0 + i, 0)),
        pl.BlockSpec((n, SSD_INNER), lambda i: (i, 0)),
        pl.BlockSpec((n, gw), lambda i: (i, SSD_INNER // gw)),
        pl.BlockSpec((n, gw), lambda i: (i, SSD_INNER // gw + 1)),
        pl.BlockSpec((4, SSD_INNER), lambda i: (0, 0)),
        pl.BlockSpec((4, gw), lambda i: (0, SSD_INNER // gw)),
        pl.BlockSpec((4, gw), lambda i: (0, SSD_INNER // gw + 1)),
        pl.BlockSpec((1, SSD_INNER), lambda i: (0, 0)),
        pl.BlockSpec((1, gw), lambda i: (0, SSD_INNER // gw)),
        pl.BlockSpec((1, gw), lambda i: (0, SSD_INNER // gw + 1)),
        pl.BlockSpec((1, LANES), lambda i: (0, 0)),
        pl.BlockSpec((1, LANES), lambda i: (0, 0)),
        pl.BlockSpec((1, SSD_INNER), lambda i: (0, 0)),
        pl.BlockSpec((1, SSD_INNER), lambda i: (0, 0)),
        pl.BlockSpec((SEQS_PER_STEP, SSD_GROUPS, SSD_GROUP_DIM, st), lambda i: (i, 0, 0, 0)),
    ]
    state_bytes = SEQS_PER_STEP * SSD_INNER * st * 4
    return pl.pallas_call(
        _ssd_sample_body, grid=(steps,), in_specs=in_specs,
        out_specs=[pl.BlockSpec((n, SSD_INNER), lambda i: (i, 0)),
                   pl.BlockSpec((SEQS_PER_STEP, SSD_GROUPS, SSD_GROUP_DIM, st), lambda i: (i, 0, 0, 0))],
        out_shape=[jax.ShapeDtypeStruct((n_seq * SAMPLE_LEN, SSD_INNER), BF16),
                   jax.ShapeDtypeStruct((n_seq, SSD_GROUPS, SSD_GROUP_DIM, st), F32)],
        compiler_params=_vmem_params(4 * state_bytes + (8 << 20), ("arbitrary",)),
        name="ssd_sample",
    )(proj, proj, proj, proj, dt_raw, prev, prev, prev, p["conv_w"], p["conv_w"], p["conv_w"], p["conv_b"],
      p["conv_b"], p["conv_b"], p["dt_bias"], p["a_log"], p["d_lanes"], p["norm"], h0)


def _ml_prompt_body(xc_ref, v_ref, o_ref, gt_ref, gb_ref, wq_ref, wk_ref, cw_ref, cb_ref, ng_ref,
                    y_ref, cs_ref, ns_ref, ms_ref, px_ref):
    c = xc_ref.shape[0]
    k, h = pl.program_id(1), pl.program_id(2)

    @pl.when(k == 0)
    def _init():
        cs_ref[0, h] = jnp.zeros((ML_V_DIM, ML_QK_DIM), F32)
        ns_ref[0, h] = jnp.zeros((1, ML_QK_DIM), F32)
        ms_ref[0, h] = jnp.zeros((1, LANES), F32)
        px_ref[h] = jnp.zeros((SUBLANES, ML_V_DIM), F32)

    u = xc_ref[...]
    xb = _conv_silu(u, px_ref[h], cw_ref[...], cb_ref[...]).astype(BF16)
    px_ref[h] = u[c - SUBLANES:]
    q = jnp.dot(xb, wq_ref[0], preferred_element_type=F32) * (ML_QK_DIM ** -0.5)
    kk = jnp.dot(xb, wk_ref[0], preferred_element_type=F32)
    q_bf, k_bf = q.astype(BF16), kk.astype(BF16)
    v = v_ref[...]

    gates = gt_ref[...] + gb_ref[...]
    lane = lax.broadcasted_iota(I32, gates.shape, 1)
    log_f = jnp.where((lane >= ML_HEADS) & (lane < 2 * ML_HEADS), -_softplus(-gates), 0.0)
    ig = _lane_col(gates, h)
    bcol = _lane_col(_cumsum_rows(log_f), ML_HEADS + h)

    row = lax.broadcasted_iota(I32, (c, c), 0)
    col = lax.broadcasted_iota(I32, (c, c), 1)
    eye, causal = row == col, col <= row
    logd = jnp.where(causal, bcol - _col_to_row(bcol, eye) + _col_to_row(ig, eye), NEG)
    m_prev = ms_ref[0, h][:, 0:1]
    inter = bcol + m_prev
    mt = jnp.maximum(jnp.max(logd, axis=1, keepdims=True), inter)
    s = _dot_t(q_bf, k_bf) * jnp.exp(logd - mt)
    wi = jnp.exp(inter - mt)
    c_old, n_old = cs_ref[0, h], ns_ref[0, h]
    num = jnp.dot(s.astype(BF16), v.astype(BF16), preferred_element_type=F32) + wi * _dot_t(q_bf, c_old.astype(BF16))
    den = jnp.sum(s, axis=1, keepdims=True) + wi * jnp.sum(q * n_old, axis=1, keepdims=True)
    hout = num / jnp.maximum(jnp.abs(den), jnp.exp(-mt))
    hout = hout * lax.rsqrt(jnp.mean(hout * hout, axis=1, keepdims=True) + EPS)
    y_ref[...] = (hout * ng_ref[...] * jax.nn.sigmoid(o_ref[...])).astype(y_ref.dtype)

    blast = bcol[c - 1:c]
    gcol = blast - bcol + ig
    carry = blast + m_prev
    m_new = jnp.maximum(carry, jnp.max(gcol, axis=0, keepdims=True))
    w_s, w_c = jnp.exp(gcol - m_new), jnp.exp(carry - m_new)
    cs_ref[0, h] = w_c * c_old + _tdot((v * w_s).astype(BF16), k_bf)
    ns_ref[0, h] = w_c * n_old + jnp.sum(w_s * kk, axis=0, keepdims=True)
    ms_ref[0, h] = jnp.broadcast_to(m_new, (1, LANES))


def _ml_prompt(proj, gates, p, n_seq, seq_len):
    c = ML_CHUNK
    nc = seq_len // c
    dv, dk = ML_V_DIM, ML_QK_DIM
    in_specs = [
        pl.BlockSpec((c, dv), lambda b, k, h: (b * nc + k, h)),
        pl.BlockSpec((c, dv), lambda b, k, h: (b * nc + k, ML_HEADS + h)),
        pl.BlockSpec((c, dv), lambda b, k, h: (b * nc + k, 2 * ML_HEADS + h)),
        pl.BlockSpec((c, LANES), lambda b, k, h: (b * nc + k, 0)),
        pl.BlockSpec((1, LANES), lambda b, k, h: (0, 0)),
        pl.BlockSpec((1, dv, dk), lambda b, k, h: (h, 0, 0)),
        pl.BlockSpec((1, dv, dk), lambda b, k, h: (h, 0, 0)),
        pl.BlockSpec((4, dv), lambda b, k, h: (0, h)),
        pl.BlockSpec((1, dv), lambda b, k, h: (0, h)),
        pl.BlockSpec((1, dv), lambda b, k, h: (0, h)),
    ]
    return pl.pallas_call(
        _ml_prompt_body, grid=(n_seq, nc, ML_HEADS), in_specs=in_specs,
        out_specs=[pl.BlockSpec((c, dv), lambda b, k, h: (b * nc + k, h)),
                   pl.BlockSpec((1, ML_HEADS, dv, dk), lambda b, k, h: (b, 0, 0, 0)),
                   pl.BlockSpec((1, ML_HEADS, 1, dk), lambda b, k, h: (b, 0, 0, 0)),
                   pl.BlockSpec((1, ML_HEADS, 1, LANES), lambda b, k, h: (b, 0, 0, 0))],
        out_shape=[jax.ShapeDtypeStruct((n_seq * seq_len, ML_INNER), BF16),
                   jax.ShapeDtypeStruct((n_seq, ML_HEADS, dv, dk), F32),
                   jax.ShapeDtypeStruct((n_seq, ML_HEADS, 1, dk), F32),
                   jax.ShapeDtypeStruct((n_seq, ML_HEADS, 1, LANES), F32)],
        scratch_shapes=[pltpu.VMEM((ML_HEADS, SUBLANES, dv), F32)],
        compiler_params=_vmem_params(4 * ML_HEADS * dv * dk * 4 + (8 << 20), ("arbitrary",) * 3),
        name="mlstm_prompt",
    )(proj, proj, proj, gates, p["gate_bias"], p["wq"], p["wk"], p["conv_w"], p["conv_b"], p["norm"])


def _ml_sample_body(xc_ref, v_ref, o_ref, gt_ref, prev_ref, gb_ref, wq_ref, wk_ref, cw_ref, cb_ref, ng_ref,
                    c0_ref, n0_ref, m0_ref, y_ref, cs_ref, ns_ref, ms_ref):
    n = SUBLANES
    row1 = lax.broadcasted_iota(I32, (n, 1), 0)
    pos1, first1 = row1 % SAMPLE_LEN, row1 < SAMPLE_LEN

    def src(v, s):
        return jnp.where(first1, v[s:s + 1], v[SAMPLE_LEN + s:SAMPLE_LEN + s + 1])

    def per_seq(vals):
        return jnp.where(first1, vals[0], vals[1])

    xs = _conv_silu_sample(xc_ref[...], prev_ref[...], cw_ref[...], cb_ref[...], pos1)
    gates = gt_ref[...] + gb_ref[...]
    lane = lax.broadcasted_iota(I32, gates.shape, 1)
    log_f = jnp.where((lane >= ML_HEADS) & (lane < 2 * ML_HEADS), -_softplus(-gates), 0.0)
    bsum = log_f + jnp.where(pos1 >= 1, pltpu.roll(log_f, 1, axis=0), 0.0)
    bsum = bsum + jnp.where(pos1 >= 2, pltpu.roll(bsum, 2, axis=0), 0.0)
    v_all, o_all = v_ref[...], o_ref[...]
    pad_rows = LANES - n

    ys = []
    for h in range(ML_HEADS):
        hs = slice(h * ML_V_DIM, (h + 1) * ML_V_DIM)
        xb = xs[:, hs].astype(BF16)
        q = jnp.dot(xb, wq_ref[h], preferred_element_type=F32) * (ML_QK_DIM ** -0.5)
        kk = jnp.dot(xb, wk_ref[h], preferred_element_type=F32)
        v = v_all[:, hs]
        ig = gates[:, h:h + 1]
        bcol = bsum[:, ML_HEADS + h:ML_HEADS + h + 1]
        c_old = [c0_ref[s_, h] for s_ in range(SEQS_PER_STEP)]
        n_old = [n0_ref[s_, h] for s_ in range(SEQS_PER_STEP)]
        m_old = [m0_ref[s_, h][:, 0:1] for s_ in range(SEQS_PER_STEP)]
        inter = bcol + per_seq(m_old)
        logd = [jnp.where(pos1 >= s, bcol - src(bcol, s) + src(ig, s), NEG) for s in range(SAMPLE_LEN)]
        mt = inter
        for ld in logd:
            mt = jnp.maximum(mt, ld)
        wi = jnp.exp(inter - mt)
        q_bf = q.astype(BF16)
        num = wi * per_seq([_dot_t(q_bf, cq.astype(BF16)) for cq in c_old])
        den = wi * jnp.sum(q * per_seq(n_old), axis=1, keepdims=True)
        for s in range(SAMPLE_LEN):
            sw = jnp.sum(q * src(kk, s), axis=1, keepdims=True) * jnp.exp(logd[s] - mt)
            num = num + sw * src(v, s)
            den = den + sw
        hout = num / jnp.maximum(jnp.abs(den), jnp.exp(-mt))
        hout = hout * lax.rsqrt(jnp.mean(hout * hout, axis=1, keepdims=True) + EPS)
        ys.append(hout * ng_ref[:, hs] * jax.nn.sigmoid(o_all[:, hs]))

        blast = src(bcol, SAMPLE_LEN - 1)
        gcol = blast - bcol + ig
        k_pad = jnp.concatenate([kk, jnp.zeros((pad_rows, ML_QK_DIM), F32)], axis=0).astype(BF16)
        for s_ in range(SEQS_PER_STEP):
            own = (row1 >= s_ * SAMPLE_LEN) & (row1 < (s_ + 1) * SAMPLE_LEN)
            last_row = (s_ + 1) * SAMPLE_LEN - 1
            carry = bcol[last_row:last_row + 1] + m_old[s_]
            m_new = jnp.maximum(carry, jnp.max(jnp.where(own, gcol, NEG), axis=0, keepdims=True))
            w_s = jnp.where(own, jnp.exp(gcol - m_new), 0.0)
            w_c = jnp.exp(carry - m_new)
            vw_pad = jnp.concatenate([v * w_s, jnp.zeros((pad_rows, ML_V_DIM), F32)], axis=0).astype(BF16)
            cs_ref[s_, h] = w_c * c_old[s_] + _tdot(vw_pad, k_pad)
            ns_ref[s_, h] = w_c * n_old[s_] + jnp.sum(w_s * kk, axis=0, keepdims=True)
            ms_ref[s_, h] = jnp.broadcast_to(m_new, (1, LANES))
    y_ref[...] = jnp.concatenate(ys, axis=1).astype(y_ref.dtype)


def _ml_sample(proj, gates, prev, c0, n0, m0, p, row0, n_seq):
    n = SUBLANES
    steps = n_seq // SEQS_PER_STEP
    r0 = row0 // n
    dv, dk = ML_V_DIM, ML_QK_DIM
    sq = SEQS_PER_STEP
    in_specs = [
        pl.BlockSpec((n, ML_INNER), lambda i: (r0 + i, 0)),
        pl.BlockSpec((n, ML_INNER), lambda i: (r0 + i, 1)),
        pl.BlockSpec((n, ML_INNER), lambda i: (r0 + i, 2)),
        pl.BlockSpec((n, LANES), lambda i: (r0 + i, 0)),
        pl.BlockSpec((n, ML_INNER), lambda i: (i, 0)),
        pl.BlockSpec((1, LANES), lambda i: (0, 0)),
        pl.BlockSpec((ML_HEADS, dv, dk), lambda i: (0, 0, 0)),
        pl.BlockSpec((ML_HEADS, dv, dk), lambda i: (0, 0, 0)),
        pl.BlockSpec((4, ML_INNER), lambda i: (0, 0)),
        pl.BlockSpec((1, ML_INNER), lambda i: (0, 0)),
        pl.BlockSpec((1, ML_INNER), lambda i: (0, 0)),
        pl.BlockSpec((sq, ML_HEADS, dv, dk), lambda i: (i, 0, 0, 0)),
        pl.BlockSpec((sq, ML_HEADS, 1, dk), lambda i: (i, 0, 0, 0)),
        pl.BlockSpec((sq, ML_HEADS, 1, LANES), lambda i: (i, 0, 0, 0)),
    ]
    state_bytes = sq * ML_HEADS * dv * dk * 4
    return pl.pallas_call(
        _ml_sample_body, grid=(steps,), in_specs=in_specs,
        out_specs=[pl.BlockSpec((n, ML_INNER), lambda i: (i, 0)),
                   pl.BlockSpec((sq, ML_HEADS, dv, dk), lambda i: (i, 0, 0, 0)),
                   pl.BlockSpec((sq, ML_HEADS, 1, dk), lambda i: (i, 0, 0, 0)),
                   pl.BlockSpec((sq, ML_HEADS, 1, LANES), lambda i: (i, 0, 0, 0))],
        out_shape=[jax.ShapeDtypeStruct((n_seq * SAMPLE_LEN, ML_INNER), BF16),
                   jax.ShapeDtypeStruct((n_seq, ML_HEADS, dv, dk), F32),
                   jax.ShapeDtypeStruct((n_seq, ML_HEADS, 1, dk), F32),
                   jax.ShapeDtypeStruct((n_seq, ML_HEADS, 1, LANES), F32)],
        compiler_params=_vmem_params(4 * state_bytes + (12 << 20), ("arbitrary",)),
        name="mlstm_sample",
    )(proj, proj, proj, gates, prev, p["gate_bias"], p["wq"], p["wk"], p["conv_w"], p["conv_b"], p["norm"],
      c0, n0, m0)


def _routing_tables(top_e, gates, n_blocks):
    n_assign = top_e.size
    e_flat = top_e.reshape(-1)
    onehot = (e_flat[:, None] == jnp.arange(N_EXPERTS, dtype=I32)[None, :]).astype(I32)
    csum = jnp.cumsum(onehot, axis=0)
    rank = jnp.sum((csum - onehot) * onehot, axis=1)
    counts = csum[-1]
    padded = (counts + MOE_ROWS - 1) // MOE_ROWS * MOE_ROWS
    pends = jnp.cumsum(padded)
    pos = jnp.sum(onehot * (pends - padded)[None, :], axis=1) + rank
    n_rows = n_blocks * MOE_ROWS
    slot_tok = jnp.zeros((n_rows,), I32).at[pos].set(jnp.arange(n_assign, dtype=I32) // TOP_K)
    slot_gate = jnp.zeros((n_rows,), F32).at[pos].set(gates.reshape(-1))
    block_start = jnp.arange(n_blocks, dtype=I32) * MOE_ROWS
    block_e = jnp.minimum(jnp.sum((pends[None, :] <= block_start[:, None]).astype(I32), axis=1), N_EXPERTS - 1)
    recast = jnp.concatenate([jnp.ones((1,), I32), (block_e[1:] != block_e[:-1]).astype(I32)])
    n_used = (pends[-1:] // MOE_ROWS).astype(I32)
    return pos, slot_tok, slot_gate, block_e, recast, n_used


def kernel(x_prompt, x_sample, state_ssm, state_ssm_conv, state_mlstm_c, state_mlstm_n, state_mlstm_m, state_mlstm_conv, norm_mix_a, ssd_w_in, ssd_conv_w, ssd_conv_b, ssd_dt_bias, ssd_a_log, ssd_d, ssd_norm, ssd_w_out, norm_ffn_a, ffn_w_gate, ffn_w_up, ffn_w_down, norm_mix_b, ml_w_in, ml_conv_w, ml_conv_b, ml_w_q, ml_w_k, ml_b_i, ml_b_f, ml_norm, ml_w_out, norm_ffn_b, moe_w_router, moe_b_router, moe_w_gate, moe_w_up, moe_w_down, final_norm):
    bp, lp, d = x_prompt.shape
    bs, ls, _ = x_sample.shape
    assert ls == SAMPLE_LEN and lp % ML_CHUNK == 0 and lp % SSD_CHUNK == 0 and bs % SEQS_PER_STEP == 0
    tp, ts = bp * lp, bs * ls
    t = tp + ts
    assert tp % ROW_TILE == 0 and ts % ROW_TILE == 0
    tall = t // 8
    assert t % (8 * 32) == 0
    x = jnp.concatenate([x_prompt.reshape(tp, d), x_sample.reshape(ts, d)], axis=0)

    def pad_lanes(v):
        return jnp.pad(v.reshape(1, -1), ((0, 0), (0, LANES - v.size)))

    def prev_rows(state):
        return jnp.pad(state, ((0, 0), (1, 0), (0, 0))).reshape(state.shape[0] * SAMPLE_LEN, state.shape[2])

    z_cols = SSD_INNER + SSD_XBC
    def last_rows(proj, lo, hi):
        rows_p = jnp.stack([proj[(b + 1) * lp - 3:(b + 1) * lp, lo:hi] for b in range(bp)])
        rows_s = proj[tp:, lo:hi].reshape(bs, ls, hi - lo)[:, ls - 3:]
        return rows_p, rows_s

    xn = _rmsnorm(x, norm_mix_a[0], BF16)
    proj = _matmul(xn, [ssd_w_in], tm=tall, tn=1024, n_cols=z_cols, out_dtype=F32, name="ssd_in")
    dt_raw = _narrow_matmul(xn, ssd_w_in, z_cols, "ssd_dt")
    ssd_p = dict(conv_w=ssd_conv_w[0], conv_b=ssd_conv_b[0].reshape(1, -1), dt_bias=pad_lanes(ssd_dt_bias[0]),
                 a_log=pad_lanes(ssd_a_log[0]), d_lanes=jnp.repeat(ssd_d[0], SSD_HEAD_DIM).reshape(1, -1),
                 norm=ssd_norm[0].reshape(1, -1))
    y_p, ssm_p = _ssd_prompt(proj, dt_raw, ssd_p, bp, lp)
    y_s, ssm_s = _ssd_sample(proj, dt_raw, prev_rows(state_ssm_conv[0]),
                             state_ssm[0].reshape(bs, SSD_GROUPS, SSD_GROUP_DIM, SSD_STATE), ssd_p, tp, bs)
    ssm_conv_p, ssm_conv_s = last_rows(proj, SSD_INNER, z_cols)
    x = _matmul(y_p, [ssd_w_out], a_tail=y_s, tm=ROW_TILE, tn=512, n_cols=d, out_dtype=F32, name="ssd_out", res=x)

    xn = _rmsnorm(x, norm_ffn_a[0], BF16)
    hid = _matmul(xn, [ffn_w_gate, ffn_w_up], tm=tall, tn=512, n_cols=ffn_w_gate.shape[2],
                  out_dtype=BF16, name="ffn_gate_up")
    x = _matmul(hid, [ffn_w_down], tm=tall // 2, tn=512, n_cols=d, out_dtype=F32, name="ffn_down", res=x)

    xn = _rmsnorm(x, norm_mix_b[0], BF16)
    ml_cols = 3 * ML_INNER
    proj2 = _matmul(xn, [ml_w_in], tm=tall, tn=1024, n_cols=ml_cols, out_dtype=F32, name="ml_in")
    gates = _narrow_matmul(xn, ml_w_in, ml_cols, "ml_gates")
    ml_p = dict(conv_w=ml_conv_w[0], conv_b=ml_conv_b[0].reshape(1, -1),
                gate_bias=pad_lanes(jnp.concatenate([ml_b_i[0], ml_b_f[0]])),
                wq=ml_w_q[0].astype(BF16), wk=ml_w_k[0].astype(BF16), norm=ml_norm[0].reshape(1, -1))
    y_p, c_p, n_p, m_p = _ml_prompt(proj2, gates, ml_p, bp, lp)
    y_s, c_s, n_s, m_s = _ml_sample(
        proj2, gates, prev_rows(state_mlstm_conv[0]), state_mlstm_c[0],
        state_mlstm_n[0].reshape(bs, ML_HEADS, 1, ML_QK_DIM),
        jnp.broadcast_to(state_mlstm_m[0][:, :, None, None], (bs, ML_HEADS, 1, LANES)), ml_p, tp, bs)
    ml_conv_p, ml_conv_s = last_rows(proj2, 0, ML_INNER)
    x = _matmul(y_p, [ml_w_out], a_tail=y_s, tm=ROW_TILE, tn=512, n_cols=d, out_dtype=F32, name="ml_out", res=x)

    top_e, top_g = _router(x, norm_ffn_b[0], moe_w_router[0], moe_b_router[0])
    n_blocks = -(-(t * TOP_K + N_EXPERTS * (MOE_ROWS - 1)) // MOE_ROWS)
    pos, slot_tok, slot_gate, block_e, recast, n_used = _routing_tables(top_e, top_g, n_blocks)
    xg = _gather_norm_rows(x, norm_ffn_b[0], slot_tok, MOE_ROWS)
    route = dict(block_e=block_e, recast=recast, n_used=n_used)
    hid = _matmul(xg, [moe_w_gate[0], moe_w_up[0]], tm=MOE_ROWS, tn=512, n_cols=moe_w_gate.shape[3],
                  out_dtype=BF16, name="moe_gate_up", **route)
    yb = _matmul(hid, [moe_w_down[0]], tm=MOE_ROWS, tn=512, n_cols=d, out_dtype=F32, name="moe_down",
                 scale=slot_gate.reshape(-1, 1), **route)
    out_p, out_s = _combine(yb, pos, x, final_norm, tp)

    return (out_p.reshape(bp, lp, d), out_s.reshape(bs, ls, d),
            ssm_p.reshape(1, bp, *state_ssm.shape[2:]), ssm_conv_p[None],
            c_p[None], n_p.reshape(1, bp, ML_HEADS, ML_QK_DIM), m_p[:, :, 0, 0][None], ml_conv_p[None],
            ssm_s.reshape(1, bs, *state_ssm.shape[2:]), ssm_conv_s[None],
            c_s[None], n_s.reshape(1, bs, ML_HEADS, ML_QK_DIM), m_s[:, :, 0, 0][None], ml_conv_s[None])
```

```python
import functools

import jax
import jax.numpy as jnp
from jax import lax
from jax.experimental import pallas as pl
from jax.experimental.pallas import tpu as pltpu

F32, BF16, I32 = jnp.float32, jnp.bfloat16, jnp.int32

EPS = 1e-6
D_MODEL = 2048
SSD_INNER = 4096
SSD_HEAD_DIM = 64
SSD_GROUPS = 8
SSD_STATE = 128
SSD_GROUP_DIM = SSD_INNER // SSD_GROUPS
SSD_HEADS_PER_GROUP = SSD_GROUP_DIM // SSD_HEAD_DIM
SSD_XBC = SSD_INNER + 2 * SSD_GROUPS * SSD_STATE
ML_HEADS = 8
ML_INNER = 4096
ML_V_DIM = 512
ML_QK_DIM = 256
N_EXPERTS = 8
TOP_K = 2

LANES = 128
SUBLANES = 8
SSD_CHUNK = 128
ML_CHUNK = 256
SAMPLE_LEN = 4
SEQS_PER_STEP = SUBLANES // SAMPLE_LEN
ROW_TILE = 512
MOE_ROWS = 512
COMBINE_ROWS = 256
NEG = -1e30
VMEM_CAP = 60 << 20


def _vmem_params(nbytes, semantics):
    return pltpu.CompilerParams(dimension_semantics=semantics,
                                vmem_limit_bytes=int(min(nbytes + (8 << 20), VMEM_CAP)))


def _softplus(x):
    return jnp.maximum(x, 0.0) + jnp.log1p(jnp.exp(-jnp.abs(x)))


def _silu(x):
    return x * jax.nn.sigmoid(x)


def _cumsum_rows(x):
    n = x.shape[0]
    row = lax.broadcasted_iota(I32, x.shape, 0)
    s = 1
    while s < n:
        x = x + jnp.where(row >= s, pltpu.roll(x, s, axis=0), 0.0)
        s *= 2
    return x


def _lane_col(x, lane_idx):
    lane = lax.broadcasted_iota(I32, x.shape, 1)
    return jnp.sum(jnp.where(lane == lane_idx, x, 0.0), axis=1, keepdims=True)


def _col_to_row(col, eye):
    return jnp.sum(jnp.where(eye, col, 0.0), axis=0, keepdims=True)


def _conv_silu(u, prev, w, b):
    c, width = u.shape
    row8 = lax.broadcasted_iota(I32, (SUBLANES, width), 0)
    acc = b + u * w[3:4]
    for d in (1, 2, 3):
        sh = pltpu.roll(u, d, axis=0)
        head = jnp.where(row8 < d, pltpu.roll(prev, d, axis=0), sh[:SUBLANES])
        sh = head if c == SUBLANES else jnp.concatenate([head, sh[SUBLANES:]], axis=0)
        acc = acc + sh * w[3 - d:4 - d]
    return _silu(acc)


def _conv_silu_sample(u, prev, w, b, pos):
    acc = b + u * w[3:4]
    for d in (1, 2, 3):
        sh = jnp.where(pos >= d, pltpu.roll(u, d, axis=0), pltpu.roll(prev, d + SAMPLE_LEN, axis=0))
        acc = acc + sh * w[3 - d:4 - d]
    return _silu(acc)


def _dot_t(a, b):
    return lax.dot_general(a, b, (((1,), (1,)), ((), ())), preferred_element_type=F32)


def _tdot(a, b):
    return lax.dot_general(a, b, (((0,), (0,)), ((), ())), preferred_element_type=F32)


def _rms(x, g):
    return x * lax.rsqrt(jnp.mean(x * x, axis=-1, keepdims=True) + EPS) * g


def _rmsnorm_body(x_ref, g_ref, o_ref):
    o_ref[...] = _rms(x_ref[...], g_ref[...]).astype(o_ref.dtype)


def _rmsnorm(x, g, out_dtype):
    t, d = x.shape
    row_spec = pl.BlockSpec((ROW_TILE, d), lambda i: (i, 0))
    return pl.pallas_call(
        _rmsnorm_body, grid=(t // ROW_TILE,),
        in_specs=[row_spec, pl.BlockSpec((1, d), lambda i: (0, 0))],
        out_specs=row_spec, out_shape=jax.ShapeDtypeStruct((t, d), out_dtype), name="rmsnorm",
    )(x, g.reshape(1, d))


def _matmul_body(be_ref, rc_ref, nu_ref, a_ref, *rest, n_w, tail_from, w_nk, has_res):
    del be_ref
    tail_ref = rest[0] if tail_from is not None else None
    rest = rest[1:] if tail_from is not None else rest
    w_refs, rest = rest[:n_w], rest[n_w:]
    res_ref = rest[0] if has_res else None
    rest = rest[1:] if has_res else rest
    o_ref, wb_refs = rest[0], rest[1:]
    b = pl.program_id(1)
    mm = _dot_t if w_nk else functools.partial(jnp.dot, preferred_element_type=F32)

    @pl.when(rc_ref[b] == 1)
    def _recast():
        for w_ref, wb_ref in zip(w_refs, wb_refs):
            wb_ref[...] = w_ref[...].astype(BF16)

    def compute(a):
        acc = mm(a, wb_refs[0][...])
        if n_w == 2:
            acc = _silu(acc) * mm(a, wb_refs[1][...])
        if has_res:
            acc = acc + res_ref[...]
        o_ref[...] = acc.astype(o_ref.dtype)

    if tail_from is None:
        pl.when(b < nu_ref[0])(lambda: compute(a_ref[...]))
    else:
        pl.when((b < nu_ref[0]) & (b < tail_from))(lambda: compute(a_ref[...]))
        pl.when((b < nu_ref[0]) & (b >= tail_from))(lambda: compute(tail_ref[...]))

    @pl.when(b >= nu_ref[0])
    def _unused():
        o_ref[...] = jnp.zeros(o_ref.shape, o_ref.dtype)


def _matmul(a, ws, *, tm, tn, n_cols, out_dtype, name, a_tail=None, block_e=None, recast=None, n_used=None,
            res=None, w_nk=False):
    k = a.shape[1]
    m = a.shape[0] + (0 if a_tail is None else a_tail.shape[0])
    nb, nj, n_w = m // tm, n_cols // tn, len(ws)
    tail_from = None if a_tail is None else a.shape[0] // tm
    if block_e is None:
        block_e = jnp.zeros((nb,), I32)
        recast = jnp.zeros((nb,), I32).at[0].set(1)
        n_used = jnp.full((1,), nb, I32)
    if a_tail is None:
        in_specs, args = [pl.BlockSpec((tm, k), lambda j, b, be, rc, nu: (b, 0))], [a]
    else:
        assert a.shape[0] % tm == 0 and a_tail.shape[0] % tm == 0
        in_specs = [pl.BlockSpec((tm, k), lambda j, b, be, rc, nu: (jnp.minimum(b, tail_from - 1), 0)),
                    pl.BlockSpec((tm, k), lambda j, b, be, rc, nu: (jnp.maximum(b - tail_from, 0), 0))]
        args = [a, a_tail]
    if w_nk:
        in_specs += [pl.BlockSpec((None, tn, k), lambda j, b, be, rc, nu: (be[b], j, 0))] * n_w
    else:
        in_specs += [pl.BlockSpec((None, k, tn), lambda j, b, be, rc, nu: (be[b], 0, j))] * n_w
    args += list(ws)
    if res is not None:
        in_specs.append(pl.BlockSpec((tm, tn), lambda j, b, be, rc, nu: (b, j)))
        args.append(res)
    out_bytes = jnp.dtype(out_dtype).itemsize
    vmem = (n_w * k * tn * (2 * 4 + 2) + (2 if a_tail is None else 4) * tm * k * 2 + 2 * tm * tn * out_bytes
            + (2 * tm * tn * 4 if res is not None else 0) + 2 * tm * tn * 4)
    grid_spec = pltpu.PrefetchScalarGridSpec(
        num_scalar_prefetch=3, grid=(nj, nb), in_specs=in_specs,
        out_specs=pl.BlockSpec((tm, tn), lambda j, b, be, rc, nu: (b, j)),
        scratch_shapes=[pltpu.VMEM((tn, k) if w_nk else (k, tn), BF16)] * n_w)
    body = functools.partial(_matmul_body, n_w=n_w, tail_from=tail_from, w_nk=w_nk, has_res=res is not None)
    return pl.pallas_call(
        body, grid_spec=grid_spec, out_shape=jax.ShapeDtypeStruct((m, n_cols), out_dtype), name=name,
        compiler_params=_vmem_params(vmem, ("arbitrary", "arbitrary")),
    )(block_e, recast, n_used, *args)


def _narrow_body(a_ref, w_ref, o_ref, *, n_valid):
    row = lax.broadcasted_iota(I32, w_ref.shape, 0)
    w = jnp.where(row < n_valid, w_ref[...], 0.0)
    o_ref[...] = _dot_t(a_ref[...], w.astype(BF16))


def _narrow_matmul(a, w_nk, col0, name):
    t, k = a.shape
    n_valid = w_nk.shape[1] - col0
    assert col0 % LANES == 0 and 0 < n_valid <= LANES
    return pl.pallas_call(
        functools.partial(_narrow_body, n_valid=n_valid), grid=(t // ROW_TILE,),
        in_specs=[pl.BlockSpec((ROW_TILE, k), lambda i: (i, 0)),
                  pl.BlockSpec((None, LANES, k), lambda i: (0, col0 // LANES, 0))],
        out_specs=pl.BlockSpec((ROW_TILE, LANES), lambda i: (i, 0)),
        out_shape=jax.ShapeDtypeStruct((t, LANES), F32), name=name,
    )(a, w_nk)


def _router_body(x_ref, ng_ref, w_ref, b_ref, e_ref, g_ref):
    xn = _rms(x_ref[...], ng_ref[...]).astype(BF16)
    logits = jnp.dot(xn, w_ref[...].astype(BF16), preferred_element_type=F32) + b_ref[...]
    lane = lax.broadcasted_iota(I32, logits.shape, 1)
    lane_f = lane.astype(F32)
    m1 = jnp.max(logits, axis=1, keepdims=True)
    i1 = jnp.min(jnp.where(logits == m1, lane_f, float(LANES)), axis=1, keepdims=True)
    rest = jnp.where(lane_f == i1, 2 * NEG, logits)
    m2 = jnp.max(rest, axis=1, keepdims=True)
    i2 = jnp.min(jnp.where(rest == m2, lane_f, float(LANES)), axis=1, keepdims=True)
    e2 = jnp.exp(m2 - m1)
    den = 1.0 + e2
    e_ref[...] = jnp.where(lane == 0, i1, jnp.where(lane == 1, i2, 0.0)).astype(I32)
    g_ref[...] = jnp.where(lane == 0, 1.0 / den, jnp.where(lane == 1, e2 / den, 0.0))


def _router(x, norm_g, w, b):
    t, k = x.shape
    wp = jnp.pad(w, ((0, 0), (0, LANES - N_EXPERTS)))
    bp = jnp.pad(b.reshape(1, N_EXPERTS), ((0, 0), (0, LANES - N_EXPERTS)), constant_values=NEG)
    e, g = pl.pallas_call(
        _router_body, grid=(t // ROW_TILE,),
        in_specs=[pl.BlockSpec((ROW_TILE, k), lambda i: (i, 0)), pl.BlockSpec((1, k), lambda i: (0, 0)),
                  pl.BlockSpec((k, LANES), lambda i: (0, 0)), pl.BlockSpec((1, LANES), lambda i: (0, 0))],
        out_specs=[pl.BlockSpec((ROW_TILE, LANES), lambda i: (i, 0))] * 2,
        out_shape=[jax.ShapeDtypeStruct((t, LANES), I32), jax.ShapeDtypeStruct((t, LANES), F32)],
        name="moe_router",
    )(x, norm_g.reshape(1, k), wp, bp)
    return e[:, :TOP_K], g


def _gather_body(tok_ref, src_ref, ng_ref, o_ref, buf_ref, sem_ref):
    b, nb = pl.program_id(0), pl.num_programs(0)
    rows = buf_ref.shape[1]

    def row_copy(blk, slot, r):
        return pltpu.make_async_copy(src_ref.at[pl.ds(tok_ref[blk * rows + r], 1)],
                                     buf_ref.at[slot, pl.ds(r, 1)], sem_ref.at[slot])

    def start_block(blk, slot):
        def body(r, carry):
            row_copy(blk, slot, r).start()
            return carry
        lax.fori_loop(0, rows, body, 0, unroll=8)

    @pl.when(b == 0)
    def _first():
        start_block(0, 0)

    @pl.when(b + 1 < nb)
    def _next():
        start_block(b + 1, (b + 1) % 2)

    slot = b % 2
    pltpu.make_async_copy(src_ref.at[pl.ds(0, rows)], buf_ref.at[slot], sem_ref.at[slot]).wait()
    o_ref[...] = _rms(buf_ref[slot], ng_ref[...]).astype(o_ref.dtype)


def _gather_norm_rows(x, norm_g, slot_tok, rows):
    n_rows, d = slot_tok.shape[0], x.shape[1]
    grid_spec = pltpu.PrefetchScalarGridSpec(
        num_scalar_prefetch=1, grid=(n_rows // rows,),
        in_specs=[pl.BlockSpec(memory_space=pl.ANY), pl.BlockSpec((1, d), lambda b, tok: (0, 0))],
        out_specs=pl.BlockSpec((rows, d), lambda b, tok: (b, 0)),
        scratch_shapes=[pltpu.VMEM((2, rows, d), F32), pltpu.SemaphoreType.DMA((2,))])
    return pl.pallas_call(
        _gather_body, grid_spec=grid_spec, out_shape=jax.ShapeDtypeStruct((n_rows, d), BF16),
        compiler_params=_vmem_params(2 * rows * d * 4 + 4 * rows * d * 4, ("arbitrary",)), name="moe_gather",
    )(slot_tok, x, norm_g.reshape(1, d))


def _combine_body(pos_ref, yb_ref, x_ref, gate_ref, g_ref, op_ref, os_ref, buf_ref, sem_ref, *, prompt_tiles):
    i, n = pl.program_id(0), pl.num_programs(0)
    tt = x_ref.shape[0]

    def row_copy(tile, slot, tl, k):
        return pltpu.make_async_copy(yb_ref.at[pl.ds(pos_ref[(tile * tt + tl) * TOP_K + k], 1)],
                                     buf_ref.at[slot, pl.ds(k * tt + tl, 1)], sem_ref.at[slot])

    def start_tile(tile, slot):
        def body(tl, carry):
            for k in range(TOP_K):
                row_copy(tile, slot, tl, k).start()
            return carry
        lax.fori_loop(0, tt, body, 0, unroll=4)

    @pl.when(i == 0)
    def _first():
        start_tile(0, 0)

    @pl.when(i + 1 < n)
    def _next():
        start_tile(i + 1, (i + 1) % 2)

    slot = i % 2
    pltpu.make_async_copy(yb_ref.at[pl.ds(0, TOP_K * tt)], buf_ref.at[slot], sem_ref.at[slot]).wait()

    gates = gate_ref[...]
    x = x_ref[...] + (gates[:, 0:1] * buf_ref[slot, :tt] + gates[:, 1:2] * buf_ref[slot, tt:])
    y = _rms(x, g_ref[...])

    @pl.when(i < prompt_tiles)
    def _prompt():
        op_ref[...] = y

    @pl.when(i >= prompt_tiles)
    def _sample():
        os_ref[...] = y


def _combine(yb, pos, gates, x, g, n_prompt):
    t, d = x.shape
    tt = COMBINE_ROWS
    pt = n_prompt // tt
    assert TOP_K == 2
    grid_spec = pltpu.PrefetchScalarGridSpec(
        num_scalar_prefetch=1, grid=(t // tt,),
        in_specs=[pl.BlockSpec(memory_space=pl.ANY), pl.BlockSpec((tt, d), lambda i, pos: (i, 0)),
                  pl.BlockSpec((tt, LANES), lambda i, pos: (i, 0)), pl.BlockSpec((1, d), lambda i, pos: (0, 0))],
        out_specs=[pl.BlockSpec((tt, d), lambda i, pos: (jnp.minimum(i, pt - 1), 0)),
                   pl.BlockSpec((tt, d), lambda i, pos: (jnp.maximum(i - pt, 0), 0))],
        scratch_shapes=[pltpu.VMEM((2, TOP_K * tt, d), F32), pltpu.SemaphoreType.DMA((2,))])
    return pl.pallas_call(
        functools.partial(_combine_body, prompt_tiles=pt), grid_spec=grid_spec,
        out_shape=[jax.ShapeDtypeStruct((n_prompt, d), F32), jax.ShapeDtypeStruct((t - n_prompt, d), F32)],
        compiler_params=_vmem_params((2 * TOP_K + 6) * tt * d * 4, ("arbitrary",)), name="moe_combine",
    )(pos, yb, x, gates, g.reshape(1, d))


def _split3(x):
    hi = x.astype(BF16)
    r1 = x - hi.astype(F32)
    mid = r1.astype(BF16)
    lo = (r1 - mid.astype(F32)).astype(BF16)
    return jnp.concatenate([hi, mid, lo], axis=1)


def _ssd_prompt_body(z_ref, xr_ref, br_ref, cr_ref, dt_ref, cwx_ref, cwb_ref, cwc_ref, cbx_ref, cbb_ref,
                     cbc_ref, dtb_ref, alog_ref, dsk_ref, ng_ref, e3_ref, y_ref, h_ref, px_ref, pb_ref, pc_ref,
                     cum_ref, cumt_ref, cumts_ref, cum3_ref, dt3_ref):
    c = xr_ref.shape[0]
    half = SSD_HEAD_DIM
    k, g = pl.program_id(1), pl.program_id(2)

    @pl.when(k == 0)
    def _init():
        h_ref[0, g] = jnp.zeros((SSD_GROUP_DIM, SSD_STATE), F32)
        px_ref[g] = jnp.zeros((SUBLANES, SSD_GROUP_DIM), F32)
        pb_ref[g] = jnp.zeros((SUBLANES, SSD_STATE), F32)
        pc_ref[g] = jnp.zeros((SUBLANES, SSD_STATE), F32)

    @pl.when(g == 0)
    def _per_chunk():
        dt = _softplus(dt_ref[...] + dtb_ref[...])
        cum = _cumsum_rows(dt * (-jnp.exp(alog_ref[...])))
        cum_t = cum.T
        cum_ref[...] = cum
        cumt_ref[...] = cum_t
        cumts_ref[...] = pltpu.roll(cum_t, half, axis=1)
        cum3_ref[...] = _split3(cum)
        dt3_ref[...] = _split3(dt)

    xr, br, cr = xr_ref[...], br_ref[...], cr_ref[...]
    xs = _conv_silu(xr, px_ref[g], cwx_ref[...], cbx_ref[...])
    bm = _conv_silu(br, pb_ref[g], cwb_ref[...], cbb_ref[...])
    cm = _conv_silu(cr, pc_ref[g], cwc_ref[...], cbc_ref[...])
    px_ref[g] = xr[c - SUBLANES:]
    pb_ref[g] = br[c - SUBLANES:]
    pc_ref[g] = cr[c - SUBLANES:]

    cum_e = jnp.dot(cum3_ref[...], e3_ref[0], preferred_element_type=F32)
    dt_e = jnp.dot(dt3_ref[...], e3_ref[0], preferred_element_type=F32)
    xd = xs * dt_e
    bm_bf, cm_bf = bm.astype(BF16), cm.astype(BF16)
    cb = _dot_t(cm_bf, bm_bf)

    lane = lax.broadcasted_iota(I32, (c, LANES), 1)
    trow = lax.broadcasted_iota(I32, (c, LANES), 0)
    low = lane < half
    s_lo = lane % half
    mask_a, mask_b = trow >= s_lo, trow >= s_lo + half
    cb_sw = pltpu.roll(cb, half, axis=1)
    cb_a, cb_b = jnp.where(low, cb, cb_sw), jnp.where(low, cb_sw, cb)
    low1 = low[0:1]

    ys = []
    for p in range(SSD_HEADS_PER_GROUP // 2):
        h0 = g * SSD_HEADS_PER_GROUP + 2 * p
        r0, r1 = cumt_ref[pl.ds(h0, 1), :], cumt_ref[pl.ds(h0 + 1, 1), :]
        r0s, r1s = cumts_ref[pl.ds(h0, 1), :], cumts_ref[pl.ds(h0 + 1, 1), :]
        row_a, row_b = jnp.where(low1, r0, r1s), jnp.where(low1, r0s, r1)
        cp = cum_e[:, p * LANES:(p + 1) * LANES]
        w_a = jnp.exp(jnp.where(mask_a, cp - row_a, NEG)) * cb_a
        w_b = jnp.exp(jnp.where(mask_b, cp - row_b, NEG)) * cb_b
        xp = xd[:, p * LANES:(p + 1) * LANES]
        top, bot = jnp.where(low, xp, 0.0), jnp.where(low, 0.0, xp)
        rhs = jnp.concatenate([top[:half], bot[:half], top[half:], bot[half:]], axis=0).astype(BF16)
        ys.append(jnp.dot(jnp.concatenate([w_a, w_b], axis=1).astype(BF16), rhs, preferred_element_type=F32))

    h_old = h_ref[0, g]
    y = jnp.concatenate(ys, axis=1) + _dot_t(cm_bf, h_old.astype(BF16)) * jnp.exp(cum_e) + dsk_ref[...] * xs
    y = y * _silu(z_ref[...])
    y = y * lax.rsqrt(jnp.mean(y * y, axis=1, keepdims=True) + EPS)
    y_ref[...] = (y * ng_ref[...]).astype(y_ref.dtype)

    tail_x = (xd * jnp.exp(cum_e[c - 1:c] - cum_e)).astype(BF16)
    upd = _tdot(tail_x, bm_bf)
    last = cum_ref[c - 1:c, :]
    for j in range(SSD_HEADS_PER_GROUP):
        rs = slice(j * half, (j + 1) * half)
        decay = jnp.exp(_lane_col(last, g * SSD_HEADS_PER_GROUP + j))
        h_ref[0, g, rs, :] = decay * h_old[rs] + upd[rs]


def _ssd_prompt(proj, dt_raw, p, n_seq, seq_len):
    c = SSD_CHUNK
    nc = seq_len // c
    gd, st = SSD_GROUP_DIM, SSD_STATE
    xo, bo, co = SSD_INNER // gd, 2 * SSD_INNER // st, (2 * SSD_INNER + SSD_GROUPS * st) // st
    cxo, cbo, cco = 0, SSD_INNER // st, (SSD_INNER + SSD_GROUPS * st) // st

    def rows(f):
        return lambda b, k, g: (b * nc + k, f(g))

    in_specs = [
        pl.BlockSpec((c, gd), rows(lambda g: g)),
        pl.BlockSpec((c, gd), rows(lambda g: xo + g)),
        pl.BlockSpec((c, st), rows(lambda g: bo + g)),
        pl.BlockSpec((c, st), rows(lambda g: co + g)),
        pl.BlockSpec((c, LANES), rows(lambda g: 0)),
        pl.BlockSpec((4, gd), lambda b, k, g: (0, cxo + g)),
        pl.BlockSpec((4, st), lambda b, k, g: (0, cbo + g)),
        pl.BlockSpec((4, st), lambda b, k, g: (0, cco + g)),
        pl.BlockSpec((1, gd), lambda b, k, g: (0, cxo + g)),
        pl.BlockSpec((1, st), lambda b, k, g: (0, cbo + g)),
        pl.BlockSpec((1, st), lambda b, k, g: (0, cco + g)),
        pl.BlockSpec((1, LANES), lambda b, k, g: (0, 0)),
        pl.BlockSpec((1, LANES), lambda b, k, g: (0, 0)),
        pl.BlockSpec((1, gd), lambda b, k, g: (0, g)),
        pl.BlockSpec((1, gd), lambda b, k, g: (0, g)),
        pl.BlockSpec((1, 3 * LANES, gd), lambda b, k, g: (g, 0, 0)),
    ]
    head = jnp.arange(LANES, dtype=I32)[None, :, None]
    chan = jnp.arange(gd, dtype=I32)[None, None, :]
    grp = jnp.arange(SSD_GROUPS, dtype=I32)[:, None, None]
    expand = (head == grp * SSD_HEADS_PER_GROUP + chan // SSD_HEAD_DIM).astype(BF16)
    expand3 = jnp.concatenate([expand] * 3, axis=1)
    assert c == LANES
    return pl.pallas_call(
        _ssd_prompt_body, grid=(n_seq, nc, SSD_GROUPS), in_specs=in_specs,
        out_specs=[pl.BlockSpec((c, gd), lambda b, k, g: (b * nc + k, g)),
                   pl.BlockSpec((1, SSD_GROUPS, gd, st), lambda b, k, g: (b, 0, 0, 0))],
        out_shape=[jax.ShapeDtypeStruct((n_seq * seq_len, SSD_INNER), BF16),
                   jax.ShapeDtypeStruct((n_seq, SSD_GROUPS, gd, st), F32)],
        scratch_shapes=[pltpu.VMEM((SSD_GROUPS, SUBLANES, gd), F32), pltpu.VMEM((SSD_GROUPS, SUBLANES, st), F32),
                        pltpu.VMEM((SSD_GROUPS, SUBLANES, st), F32),
                        pltpu.VMEM((c, LANES), F32), pltpu.VMEM((LANES, c), F32), pltpu.VMEM((LANES, c), F32),
                        pltpu.VMEM((c, 3 * LANES), BF16), pltpu.VMEM((c, 3 * LANES), BF16)],
        compiler_params=pltpu.CompilerParams(dimension_semantics=("arbitrary",) * 3),
        name="ssd_prompt",
    )(proj, proj, proj, proj, dt_raw, p["conv_w"], p["conv_w"], p["conv_w"], p["conv_b"], p["conv_b"],
      p["conv_b"], p["dt_bias"], p["a_log"], p["d_lanes"], p["norm"], expand3)


def _expand_heads(v, first_head, n_heads):
    lane = lax.broadcasted_iota(I32, (v.shape[0], LANES), 1)
    low = lane < SSD_HEAD_DIM
    out = []
    for q in range(n_heads // 2):
        j = first_head + 2 * q
        out.append(jnp.where(low, v[:, j:j + 1], v[:, j + 1:j + 2]))
    return jnp.concatenate(out, axis=1)


def _ssd_sample_body(z_ref, xr_ref, br_ref, cr_ref, dt_ref, prevx_ref, prevb_ref, prevc_ref, cwx_ref, cwb_ref,
                     cwc_ref, cbx_ref, cbb_ref, cbc_ref, dtb_ref, alog_ref, dsk_ref, ng_ref, h0_ref,
                     y_ref, h_ref):
    n = SUBLANES
    row1 = lax.broadcasted_iota(I32, (n, 1), 0)
    pos1, first1 = row1 % SAMPLE_LEN, row1 < SAMPLE_LEN

    def src(v, s):
        return jnp.where(first1, v[s:s + 1], v[SAMPLE_LEN + s:SAMPLE_LEN + s + 1])

    xs = _conv_silu_sample(xr_ref[...], prevx_ref[...], cwx_ref[...], cbx_ref[...], pos1)
    bm = _conv_silu_sample(br_ref[...], prevb_ref[...], cwb_ref[...], cbb_ref[...], pos1)
    cm = _conv_silu_sample(cr_ref[...], prevc_ref[...], cwc_ref[...], cbc_ref[...], pos1)
    dt = _softplus(dt_ref[...] + dtb_ref[...])
    la = dt * (-jnp.exp(alog_ref[...]))
    cum = la + jnp.where(pos1 >= 1, pltpu.roll(la, 1, axis=0), 0.0)
    cum = cum + jnp.where(pos1 >= 2, pltpu.roll(cum, 2, axis=0), 0.0)
    z = z_ref[...]
    pad_rows = LANES - n
    hrow = lax.broadcasted_iota(I32, (SSD_GROUP_DIM, SSD_STATE), 0) // SSD_HEAD_DIM

    ys = []
    for g in range(SSD_GROUPS):
        gs = slice(g * SSD_GROUP_DIM, (g + 1) * SSD_GROUP_DIM)
        ns = slice(g * SSD_STATE, (g + 1) * SSD_STATE)
        x_g, b_g, c_g = xs[:, gs], bm[:, ns], cm[:, ns]
        cum_e = _expand_heads(cum, g * SSD_HEADS_PER_GROUP, SSD_HEADS_PER_GROUP)
        dt_e = _expand_heads(dt, g * SSD_HEADS_PER_GROUP, SSD_HEADS_PER_GROUP)
        last_e = src(cum_e, SAMPLE_LEN - 1)
        y = jnp.zeros((n, SSD_GROUP_DIM), F32)
        for s in range(SAMPLE_LEN):
            cb_s = jnp.sum(c_g * src(b_g, s), axis=1, keepdims=True)
            decay = jnp.exp(jnp.where(pos1 >= s, cum_e - src(cum_e, s), NEG))
            y = y + decay * cb_s * src(dt_e, s) * src(x_g, s)
        c_bf = c_g.astype(BF16)
        h_old = [h0_ref[q, g] for q in range(SEQS_PER_STEP)]
        y_inter = jnp.where(first1, _dot_t(c_bf, h_old[0].astype(BF16)), _dot_t(c_bf, h_old[1].astype(BF16)))
        y = y + y_inter * jnp.exp(cum_e) + dsk_ref[:, gs] * x_g
        y = y * _silu(z[:, gs])
        y = y * lax.rsqrt(jnp.mean(y * y, axis=1, keepdims=True) + EPS)
        ys.append(y * ng_ref[:, gs])
        xw = x_g * jnp.exp(last_e - cum_e) * dt_e
        b_pad = jnp.concatenate([b_g, jnp.zeros((pad_rows, SSD_STATE), F32)], axis=0).astype(BF16)
        for q in range(SEQS_PER_STEP):
            own = (row1 >= q * SAMPLE_LEN) & (row1 < (q + 1) * SAMPLE_LEN)
            xw_pad = jnp.concatenate([jnp.where(own, xw, 0.0), jnp.zeros((pad_rows, SSD_GROUP_DIM), F32)],
                                     axis=0).astype(BF16)
            last_row = (q + 1) * SAMPLE_LEN - 1
            elast = jnp.exp(cum[last_row:last_row + 1])
            dec = jnp.zeros((SSD_GROUP_DIM, SSD_STATE), F32)
            for j in range(SSD_HEADS_PER_GROUP):
                hd = g * SSD_HEADS_PER_GROUP + j
                dec = jnp.where(hrow == j, elast[:, hd:hd + 1], dec)
            h_ref[q, g] = dec * h_old[q] + _tdot(xw_pad, b_pad)
    y_ref[...] = jnp.concatenate(ys, axis=1).astype(y_ref.dtype)


def _ssd_sample(proj, dt_raw, prev, h0, p, row0, n_seq):
    n = SUBLANES
    steps = n_seq // SEQS_PER_STEP
    r0 = row0 // n
    st = SSD_STATE
    zo, xo, bo, co = 0, 1, 2 * SSD_INNER // (SSD_GROUPS * st), (2 * SSD_INNER + SSD_GROUPS * st) // (SSD_GROUPS * st)
    gw = SSD_GROUPS * st
    in_specs = [
        pl.BlockSpec((n, SSD_INNER), lambda i: (r0 + i, zo)),
        pl.BlockSpec((n, SSD_INNER), lambda i: (r0 + i, xo)),
        pl.BlockSpec((n, gw), lambda i: (r0 + i, bo)),
        pl.BlockSpec((n, gw), lambda i: (r0 + i, co)),
        pl.BlockSpec((n, LANES), lambda i: (r0 + i, 0)),
        pl.BlockSpec((n, SSD_INNER), lambda i: (i, 0)),
        pl.BlockSpec((n, gw), lambda i: (i, SSD_INNER // gw)),
        pl.BlockSpec((n, gw), lambda i: (i, SSD_INNER // gw + 1)),
        pl.BlockSpec((4, SSD_INNER), lambda i: (0, 0)),
        pl.BlockSpec((4, gw), lambda i: (0, SSD_INNER // gw)),
        pl.BlockSpec((4, gw), lambda i: (0, SSD_INNER // gw + 1)),
        pl.BlockSpec((1, SSD_INNER), lambda i: (0, 0)),
        pl.BlockSpec((1, gw), lambda i: (0, SSD_INNER // gw)),
        pl.BlockSpec((1, gw), lambda i: (0, SSD_INNER // gw + 1)),
        pl.BlockSpec((1, LANES), lambda i: (0, 0)),
        pl.BlockSpec((1, LANES), lambda i: (0, 0)),
        pl.BlockSpec((1, SSD_INNER), lambda i: (0, 0)),
        pl.BlockSpec((1, SSD_INNER), lambda i: (0, 0)),
        pl.BlockSpec((SEQS_PER_STEP, SSD_GROUPS, SSD_GROUP_DIM, st), lambda i: (i, 0, 0, 0)),
    ]
    state_bytes = SEQS_PER_STEP * SSD_INNER * st * 4
    return pl.pallas_call(
        _ssd_sample_body, grid=(steps,), in_specs=in_specs,
        out_specs=[pl.BlockSpec((n, SSD_INNER), lambda i: (i, 0)),
                   pl.BlockSpec((SEQS_PER_STEP, SSD_GROUPS, SSD_GROUP_DIM, st), lambda i: (i, 0, 0, 0))],
        out_shape=[jax.ShapeDtypeStruct((n_seq * SAMPLE_LEN, SSD_INNER), BF16),
                   jax.ShapeDtypeStruct((n_seq, SSD_GROUPS, SSD_GROUP_DIM, st), F32)],
        compiler_params=_vmem_params(4 * state_bytes + (8 << 20), ("arbitrary",)),
        name="ssd_sample",
    )(proj, proj, proj, proj, dt_raw, prev, prev, prev, p["conv_w"], p["conv_w"], p["conv_w"], p["conv_b"],
      p["conv_b"], p["conv_b"], p["dt_bias"], p["a_log"], p["d_lanes"], p["norm"], h0)


def _ml_prompt_body(xc_ref, v_ref, o_ref, gt_ref, gb_ref, wq_ref, wk_ref, cw_ref, cb_ref, ng_ref,
                    y_ref, cs_ref, ns_ref, ms_ref, px_ref):
    c = xc_ref.shape[0]
    k, h = pl.program_id(1), pl.program_id(2)

    @pl.when(k == 0)
    def _init():
        cs_ref[0, h] = jnp.zeros((ML_V_DIM, ML_QK_DIM), F32)
        ns_ref[0, h] = jnp.zeros((1, ML_QK_DIM), F32)
        ms_ref[0, h] = jnp.zeros((1, LANES), F32)
        px_ref[h] = jnp.zeros((SUBLANES, ML_V_DIM), F32)

    u = xc_ref[...]
    xb = _conv_silu(u, px_ref[h], cw_ref[...], cb_ref[...]).astype(BF16)
    px_ref[h] = u[c - SUBLANES:]
    q = jnp.dot(xb, wq_ref[0], preferred_element_type=F32) * (ML_QK_DIM ** -0.5)
    kk = jnp.dot(xb, wk_ref[0], preferred_element_type=F32)
    q_bf, k_bf = q.astype(BF16), kk.astype(BF16)
    v = v_ref[...]

    gates = gt_ref[...] + gb_ref[...]
    lane = lax.broadcasted_iota(I32, gates.shape, 1)
    log_f = jnp.where((lane >= ML_HEADS) & (lane < 2 * ML_HEADS), -_softplus(-gates), 0.0)
    ig = _lane_col(gates, h)
    bcol = _lane_col(_cumsum_rows(log_f), ML_HEADS + h)

    row = lax.broadcasted_iota(I32, (c, c), 0)
    col = lax.broadcasted_iota(I32, (c, c), 1)
    eye, causal = row == col, col <= row
    logd = jnp.where(causal, bcol - _col_to_row(bcol, eye) + _col_to_row(ig, eye), NEG)
    m_prev = ms_ref[0, h][:, 0:1]
    inter = bcol + m_prev
    mt = jnp.maximum(jnp.max(logd, axis=1, keepdims=True), inter)
    s = _dot_t(q_bf, k_bf) * jnp.exp(logd - mt)
    wi = jnp.exp(inter - mt)
    c_old, n_old = cs_ref[0, h], ns_ref[0, h]
    num = jnp.dot(s.astype(BF16), v.astype(BF16), preferred_element_type=F32) + wi * _dot_t(q_bf, c_old.astype(BF16))
    den = jnp.sum(s, axis=1, keepdims=True) + wi * jnp.sum(q * n_old, axis=1, keepdims=True)
    hout = num / jnp.maximum(jnp.abs(den), jnp.exp(-mt))
    hout = hout * lax.rsqrt(jnp.mean(hout * hout, axis=1, keepdims=True) + EPS)
    y_ref[...] = (hout * ng_ref[...] * jax.nn.sigmoid(o_ref[...])).astype(y_ref.dtype)

    blast = bcol[c - 1:c]
    gcol = blast - bcol + ig
    carry = blast + m_prev
    m_new = jnp.maximum(carry, jnp.max(gcol, axis=0, keepdims=True))
    w_s, w_c = jnp.exp(gcol - m_new), jnp.exp(carry - m_new)
    cs_ref[0, h] = w_c * c_old + _tdot((v * w_s).astype(BF16), k_bf)
    ns_ref[0, h] = w_c * n_old + jnp.sum(w_s * kk, axis=0, keepdims=True)
    ms_ref[0, h] = jnp.broadcast_to(m_new, (1, LANES))


def _ml_prompt(proj, gates, p, n_seq, seq_len):
    c = ML_CHUNK
    nc = seq_len // c
    dv, dk = ML_V_DIM, ML_QK_DIM
    in_specs = [
        pl.BlockSpec((c, dv), lambda b, k, h: (b * nc + k, h)),
        pl.BlockSpec((c, dv), lambda b, k, h: (b * nc + k, ML_HEADS + h)),
        pl.BlockSpec((c, dv), lambda b, k, h: (b * nc + k, 2 * ML_HEADS + h)),
        pl.BlockSpec((c, LANES), lambda b, k, h: (b * nc + k, 0)),
        pl.BlockSpec((1, LANES), lambda b, k, h: (0, 0)),
        pl.BlockSpec((1, dv, dk), lambda b, k, h: (h, 0, 0)),
        pl.BlockSpec((1, dv, dk), lambda b, k, h: (h, 0, 0)),
        pl.BlockSpec((4, dv), lambda b, k, h: (0, h)),
        pl.BlockSpec((1, dv), lambda b, k, h: (0, h)),
        pl.BlockSpec((1, dv), lambda b, k, h: (0, h)),
    ]
    return pl.pallas_call(
        _ml_prompt_body, grid=(n_seq, nc, ML_HEADS), in_specs=in_specs,
        out_specs=[pl.BlockSpec((c, dv), lambda b, k, h: (b * nc + k, h)),
                   pl.BlockSpec((1, ML_HEADS, dv, dk), lambda b, k, h: (b, 0, 0, 0)),
                   pl.BlockSpec((1, ML_HEADS, 1, dk), lambda b, k, h: (b, 0, 0, 0)),
                   pl.BlockSpec((1, ML_HEADS, 1, LANES), lambda b, k, h: (b, 0, 0, 0))],
        out_shape=[jax.ShapeDtypeStruct((n_seq * seq_len, ML_INNER), BF16),
                   jax.ShapeDtypeStruct((n_seq, ML_HEADS, dv, dk), F32),
                   jax.ShapeDtypeStruct((n_seq, ML_HEADS, 1, dk), F32),
                   jax.ShapeDtypeStruct((n_seq, ML_HEADS, 1, LANES), F32)],
        scratch_shapes=[pltpu.VMEM((ML_HEADS, SUBLANES, dv), F32)],
        compiler_params=_vmem_params(4 * ML_HEADS * dv * dk * 4 + (8 << 20), ("arbitrary",) * 3),
        name="mlstm_prompt",
    )(proj, proj, proj, gates, p["gate_bias"], p["wq"], p["wk"], p["conv_w"], p["conv_b"], p["norm"])


def _ml_sample_body(xc_ref, v_ref, o_ref, gt_ref, prev_ref, gb_ref, wq_ref, wk_ref, cw_ref, cb_ref, ng_ref,
                    c0_ref, n0_ref, m0_ref, y_ref, cs_ref, ns_ref, ms_ref):
    n = SUBLANES
    row1 = lax.broadcasted_iota(I32, (n, 1), 0)
    pos1, first1 = row1 % SAMPLE_LEN, row1 < SAMPLE_LEN

    def src(v, s):
        return jnp.where(first1, v[s:s + 1], v[SAMPLE_LEN + s:SAMPLE_LEN + s + 1])

    def per_seq(vals):
        return jnp.where(first1, vals[0], vals[1])

    xs = _conv_silu_sample(xc_ref[...], prev_ref[...], cw_ref[...], cb_ref[...], pos1)
    gates = gt_ref[...] + gb_ref[...]
    lane = lax.broadcasted_iota(I32, gates.shape, 1)
    log_f = jnp.where((lane >= ML_HEADS) & (lane < 2 * ML_HEADS), -_softplus(-gates), 0.0)
    bsum = log_f + jnp.where(pos1 >= 1, pltpu.roll(log_f, 1, axis=0), 0.0)
    bsum = bsum + jnp.where(pos1 >= 2, pltpu.roll(bsum, 2, axis=0), 0.0)
    v_all, o_all = v_ref[...], o_ref[...]
    pad_rows = LANES - n

    ys = []
    for h in range(ML_HEADS):
        hs = slice(h * ML_V_DIM, (h + 1) * ML_V_DIM)
        xb = xs[:, hs].astype(BF16)
        q = jnp.dot(xb, wq_ref[h], preferred_element_type=F32) * (ML_QK_DIM ** -0.5)
        kk = jnp.dot(xb, wk_ref[h], preferred_element_type=F32)
        v = v_all[:, hs]
        ig = gates[:, h:h + 1]
        bcol = bsum[:, ML_HEADS + h:ML_HEADS + h + 1]
        c_old = [c0_ref[s_, h] for s_ in range(SEQS_PER_STEP)]
        n_old = [n0_ref[s_, h] for s_ in range(SEQS_PER_STEP)]
        m_old = [m0_ref[s_, h][:, 0:1] for s_ in range(SEQS_PER_STEP)]
        inter = bcol + per_seq(m_old)
        logd = [jnp.where(pos1 >= s, bcol - src(bcol, s) + src(ig, s), NEG) for s in range(SAMPLE_LEN)]
        mt = inter
        for ld in logd:
            mt = jnp.maximum(mt, ld)
        wi = jnp.exp(inter - mt)
        q_bf = q.astype(BF16)
        num = wi * per_seq([_dot_t(q_bf, cq.astype(BF16)) for cq in c_old])
        den = wi * jnp.sum(q * per_seq(n_old), axis=1, keepdims=True)
        for s in range(SAMPLE_LEN):
            sw = jnp.sum(q * src(kk, s), axis=1, keepdims=True) * jnp.exp(logd[s] - mt)
            num = num + sw * src(v, s)
            den = den + sw
        hout = num / jnp.maximum(jnp.abs(den), jnp.exp(-mt))
        hout = hout * lax.rsqrt(jnp.mean(hout * hout, axis=1, keepdims=True) + EPS)
        ys.append(hout * ng_ref[:, hs] * jax.nn.sigmoid(o_all[:, hs]))

        blast = src(bcol, SAMPLE_LEN - 1)
        gcol = blast - bcol + ig
        k_pad = jnp.concatenate([kk, jnp.zeros((pad_rows, ML_QK_DIM), F32)], axis=0).astype(BF16)
        for s_ in range(SEQS_PER_STEP):
            own = (row1 >= s_ * SAMPLE_LEN) & (row1 < (s_ + 1) * SAMPLE_LEN)
            last_row = (s_ + 1) * SAMPLE_LEN - 1
            carry = bcol[last_row:last_row + 1] + m_old[s_]
            m_new = jnp.maximum(carry, jnp.max(jnp.where(own, gcol, NEG), axis=0, keepdims=True))
            w_s = jnp.where(own, jnp.exp(gcol - m_new), 0.0)
            w_c = jnp.exp(carry - m_new)
            vw_pad = jnp.concatenate([v * w_s, jnp.zeros((pad_rows, ML_V_DIM), F32)], axis=0).astype(BF16)
            cs_ref[s_, h] = w_c * c_old[s_] + _tdot(vw_pad, k_pad)
            ns_ref[s_, h] = w_c * n_old[s_] + jnp.sum(w_s * kk, axis=0, keepdims=True)
            ms_ref[s_, h] = jnp.broadcast_to(m_new, (1, LANES))
    y_ref[...] = jnp.concatenate(ys, axis=1).astype(y_ref.dtype)


def _ml_sample(proj, gates, prev, c0, n0, m0, p, row0, n_seq):
    n = SUBLANES
    steps = n_seq // SEQS_PER_STEP
    r0 = row0 // n
    dv, dk = ML_V_DIM, ML_QK_DIM
    sq = SEQS_PER_STEP
    in_specs = [
        pl.BlockSpec((n, ML_INNER), lambda i: (r0 + i, 0)),
        pl.BlockSpec((n, ML_INNER), lambda i: (r0 + i, 1)),
        pl.BlockSpec((n, ML_INNER), lambda i: (r0 + i, 2)),
        pl.BlockSpec((n, LANES), lambda i: (r0 + i, 0)),
        pl.BlockSpec((n, ML_INNER), lambda i: (i, 0)),
        pl.BlockSpec((1, LANES), lambda i: (0, 0)),
        pl.BlockSpec((ML_HEADS, dv, dk), lambda i: (0, 0, 0)),
        pl.BlockSpec((ML_HEADS, dv, dk), lambda i: (0, 0, 0)),
        pl.BlockSpec((4, ML_INNER), lambda i: (0, 0)),
        pl.BlockSpec((1, ML_INNER), lambda i: (0, 0)),
        pl.BlockSpec((1, ML_INNER), lambda i: (0, 0)),
        pl.BlockSpec((sq, ML_HEADS, dv, dk), lambda i: (i, 0, 0, 0)),
        pl.BlockSpec((sq, ML_HEADS, 1, dk), lambda i: (i, 0, 0, 0)),
        pl.BlockSpec((sq, ML_HEADS, 1, LANES), lambda i: (i, 0, 0, 0)),
    ]
    state_bytes = sq * ML_HEADS * dv * dk * 4
    return pl.pallas_call(
        _ml_sample_body, grid=(steps,), in_specs=in_specs,
        out_specs=[pl.BlockSpec((n, ML_INNER), lambda i: (i, 0)),
                   pl.BlockSpec((sq, ML_HEADS, dv, dk), lambda i: (i, 0, 0, 0)),
                   pl.BlockSpec((sq, ML_HEADS, 1, dk), lambda i: (i, 0, 0, 0)),
                   pl.BlockSpec((sq, ML_HEADS, 1, LANES), lambda i: (i, 0, 0, 0))],
        out_shape=[jax.ShapeDtypeStruct((n_seq * SAMPLE_LEN, ML_INNER), BF16),
                   jax.ShapeDtypeStruct((n_seq, ML_HEADS, dv, dk), F32),
                   jax.ShapeDtypeStruct((n_seq, ML_HEADS, 1, dk), F32),
                   jax.ShapeDtypeStruct((n_seq, ML_HEADS, 1, LANES), F32)],
        compiler_params=_vmem_params(4 * state_bytes + (12 << 20), ("arbitrary",)),
        name="mlstm_sample",
    )(proj, proj, proj, gates, prev, p["gate_bias"], p["wq"], p["wk"], p["conv_w"], p["conv_b"], p["norm"],
      c0, n0, m0)


def _routing_tables(top_e, n_blocks):
    n_assign = top_e.size
    e_flat = top_e.reshape(-1)
    onehot = (e_flat[:, None] == jnp.arange(N_EXPERTS, dtype=I32)[None, :]).astype(I32)
    csum = jnp.cumsum(onehot, axis=0)
    rank = jnp.sum((csum - onehot) * onehot, axis=1)
    counts = csum[-1]
    padded = (counts + MOE_ROWS - 1) // MOE_ROWS * MOE_ROWS
    pends = jnp.cumsum(padded)
    pos = jnp.sum(onehot * (pends - padded)[None, :], axis=1) + rank
    n_rows = n_blocks * MOE_ROWS
    slot_tok = jnp.zeros((n_rows,), I32).at[pos].set(jnp.arange(n_assign, dtype=I32) // TOP_K)
    block_start = jnp.arange(n_blocks, dtype=I32) * MOE_ROWS
    block_e = jnp.minimum(jnp.sum((pends[None, :] <= block_start[:, None]).astype(I32), axis=1), N_EXPERTS - 1)
    recast = jnp.concatenate([jnp.ones((1,), I32), (block_e[1:] != block_e[:-1]).astype(I32)])
    n_used = (pends[-1:] // MOE_ROWS).astype(I32)
    return pos, slot_tok, block_e, recast, n_used


def kernel(x_prompt, x_sample, state_ssm, state_ssm_conv, state_mlstm_c, state_mlstm_n, state_mlstm_m, state_mlstm_conv, norm_mix_a, ssd_w_in, ssd_conv_w, ssd_conv_b, ssd_dt_bias, ssd_a_log, ssd_d, ssd_norm, ssd_w_out, norm_ffn_a, ffn_w_gate, ffn_w_up, ffn_w_down, norm_mix_b, ml_w_in, ml_conv_w, ml_conv_b, ml_w_q, ml_w_k, ml_b_i, ml_b_f, ml_norm, ml_w_out, norm_ffn_b, moe_w_router, moe_b_router, moe_w_gate, moe_w_up, moe_w_down, final_norm):
    bp, lp, d = x_prompt.shape
    bs, ls, _ = x_sample.shape
    assert ls == SAMPLE_LEN and lp % ML_CHUNK == 0 and lp % SSD_CHUNK == 0 and bs % SEQS_PER_STEP == 0
    tp, ts = bp * lp, bs * ls
    t = tp + ts
    assert tp % ROW_TILE == 0 and ts % ROW_TILE == 0
    tall = t // 8
    assert t % (8 * 32) == 0
    x = jnp.concatenate([x_prompt.reshape(tp, d), x_sample.reshape(ts, d)], axis=0)

    def pad_lanes(v):
        return jnp.pad(v.reshape(1, -1), ((0, 0), (0, LANES - v.size)))

    def prev_rows(state):
        return jnp.pad(state, ((0, 0), (1, 0), (0, 0))).reshape(state.shape[0] * SAMPLE_LEN, state.shape[2])

    z_cols = SSD_INNER + SSD_XBC
    def last_rows(proj, lo, hi):
        rows_p = jnp.stack([proj[(b + 1) * lp - 3:(b + 1) * lp, lo:hi] for b in range(bp)])
        rows_s = proj[tp:, lo:hi].reshape(bs, ls, hi - lo)[:, ls - 3:]
        return rows_p, rows_s

    xn = _rmsnorm(x, norm_mix_a[0], BF16)
    ssd_w_in_nk = jnp.swapaxes(ssd_w_in, 1, 2)
    proj = _matmul(xn, [ssd_w_in_nk], tm=tall, tn=1024, n_cols=z_cols, out_dtype=F32, name="ssd_in", w_nk=True)
    dt_raw = _narrow_matmul(xn, ssd_w_in_nk, z_cols, "ssd_dt")
    ssd_p = dict(conv_w=ssd_conv_w[0], conv_b=ssd_conv_b[0].reshape(1, -1), dt_bias=pad_lanes(ssd_dt_bias[0]),
                 a_log=pad_lanes(ssd_a_log[0]), d_lanes=jnp.repeat(ssd_d[0], SSD_HEAD_DIM).reshape(1, -1),
                 norm=ssd_norm[0].reshape(1, -1))
    y_p, ssm_p = _ssd_prompt(proj, dt_raw, ssd_p, bp, lp)
    y_s, ssm_s = _ssd_sample(proj, dt_raw, prev_rows(state_ssm_conv[0]),
                             state_ssm[0].reshape(bs, SSD_GROUPS, SSD_GROUP_DIM, SSD_STATE), ssd_p, tp, bs)
    ssm_conv_p, ssm_conv_s = last_rows(proj, SSD_INNER, z_cols)
    x = _matmul(y_p, [ssd_w_out], a_tail=y_s, tm=ROW_TILE, tn=512, n_cols=d, out_dtype=F32, name="ssd_out", res=x)

    xn = _rmsnorm(x, norm_ffn_a[0], BF16)
    hid = _matmul(xn, [ffn_w_gate, ffn_w_up], tm=tall, tn=512, n_cols=ffn_w_gate.shape[2],
                  out_dtype=BF16, name="ffn_gate_up")
    x = _matmul(hid, [ffn_w_down], tm=tall // 2, tn=512, n_cols=d, out_dtype=F32, name="ffn_down", res=x)

    xn = _rmsnorm(x, norm_mix_b[0], BF16)
    ml_cols = 3 * ML_INNER
    ml_w_in_nk = jnp.swapaxes(ml_w_in, 1, 2)
    proj2 = _matmul(xn, [ml_w_in_nk], tm=tall, tn=1024, n_cols=ml_cols, out_dtype=F32, name="ml_in", w_nk=True)
    gates = _narrow_matmul(xn, ml_w_in_nk, ml_cols, "ml_gates")
    ml_p = dict(conv_w=ml_conv_w[0], conv_b=ml_conv_b[0].reshape(1, -1),
                gate_bias=pad_lanes(jnp.concatenate([ml_b_i[0], ml_b_f[0]])),
                wq=ml_w_q[0].astype(BF16), wk=ml_w_k[0].astype(BF16), norm=ml_norm[0].reshape(1, -1))
    y_p, c_p, n_p, m_p = _ml_prompt(proj2, gates, ml_p, bp, lp)
    y_s, c_s, n_s, m_s = _ml_sample(
        proj2, gates, prev_rows(state_mlstm_conv[0]), state_mlstm_c[0],
        state_mlstm_n[0].reshape(bs, ML_HEADS, 1, ML_QK_DIM),
        jnp.broadcast_to(state_mlstm_m[0][:, :, None, None], (bs, ML_HEADS, 1, LANES)), ml_p, tp, bs)
    ml_conv_p, ml_conv_s = last_rows(proj2, 0, ML_INNER)
    x = _matmul(y_p, [ml_w_out], a_tail=y_s, tm=ROW_TILE, tn=512, n_cols=d, out_dtype=F32, name="ml_out", res=x)

    top_e, top_g = _router(x, norm_ffn_b[0], moe_w_router[0], moe_b_router[0])
    n_blocks = -(-(t * TOP_K + N_EXPERTS * (MOE_ROWS - 1)) // MOE_ROWS)
    pos, slot_tok, block_e, recast, n_used = _routing_tables(top_e, n_blocks)
    xg = _gather_norm_rows(x, norm_ffn_b[0], slot_tok, MOE_ROWS)
    route = dict(block_e=block_e, recast=recast, n_used=n_used)
    hid = _matmul(xg, [moe_w_gate[0], moe_w_up[0]], tm=MOE_ROWS, tn=512, n_cols=moe_w_gate.shape[3],
                  out_dtype=BF16, name="moe_gate_up", **route)
    yb = _matmul(hid, [moe_w_down[0]], tm=MOE_ROWS, tn=512, n_cols=d, out_dtype=F32, name="moe_down", **route)
    out_p, out_s = _combine(yb, pos, top_g, x, final_norm, tp)

    return (out_p.reshape(bp, lp, d), out_s.reshape(bs, ls, d),
            ssm_p.reshape(1, bp, *state_ssm.shape[2:]), ssm_conv_p[None],
            c_p[None], n_p.reshape(1, bp, ML_HEADS, ML_QK_DIM), m_p[:, :, 0, 0][None], ml_conv_p[None],
            ssm_s.reshape(1, bs, *state_ssm.shape[2:]), ssm_conv_s[None],
            c_s[None], n_s.reshape(1, bs, ML_HEADS, ML_QK_DIM), m_s[:, :, 0, 0][None], ml_conv_s[None])
```

```python
import functools

import jax
import jax.numpy as jnp
from jax import lax
from jax.experimental import pallas as pl
from jax.experimental.pallas import tpu as pltpu

F32, BF16, I32 = jnp.float32, jnp.bfloat16, jnp.int32

EPS = 1e-6
D_MODEL = 2048
SSD_INNER = 4096
SSD_HEAD_DIM = 64
SSD_GROUPS = 8
SSD_STATE = 128
SSD_GROUP_DIM = SSD_INNER // SSD_GROUPS
SSD_HEADS_PER_GROUP = SSD_GROUP_DIM // SSD_HEAD_DIM
SSD_XBC = SSD_INNER + 2 * SSD_GROUPS * SSD_STATE
ML_HEADS = 8
ML_INNER = 4096
ML_V_DIM = 512
ML_QK_DIM = 256
N_EXPERTS = 8
TOP_K = 2

LANES = 128
SUBLANES = 8
SSD_CHUNK = 128
ML_CHUNK = 256
SAMPLE_LEN = 4
SEQS_PER_STEP = SUBLANES // SAMPLE_LEN
ROW_TILE = 512
MOE_ROWS = 512
COMBINE_ROWS = 256
DMA_QUEUES = 2
NEG = -1e30
VMEM_CAP = 60 << 20


def _vmem_params(nbytes, semantics):
    return pltpu.CompilerParams(dimension_semantics=semantics,
                                vmem_limit_bytes=int(min(nbytes + (8 << 20), VMEM_CAP)))


def _softplus(x):
    return jnp.maximum(x, 0.0) + jnp.log1p(jnp.exp(-jnp.abs(x)))


def _silu(x):
    return x * jax.nn.sigmoid(x)


def _cumsum_rows(x):
    n = x.shape[0]
    row = lax.broadcasted_iota(I32, x.shape, 0)
    s = 1
    while s < n:
        x = x + jnp.where(row >= s, pltpu.roll(x, s, axis=0), 0.0)
        s *= 2
    return x


def _lane_col(x, lane_idx):
    lane = lax.broadcasted_iota(I32, x.shape, 1)
    return jnp.sum(jnp.where(lane == lane_idx, x, 0.0), axis=1, keepdims=True)


def _col_to_row(col, eye):
    return jnp.sum(jnp.where(eye, col, 0.0), axis=0, keepdims=True)


def _conv_silu(u, prev, w, b):
    c, width = u.shape
    row8 = lax.broadcasted_iota(I32, (SUBLANES, width), 0)
    acc = b + u * w[3:4]
    for d in (1, 2, 3):
        sh = pltpu.roll(u, d, axis=0)
        head = jnp.where(row8 < d, pltpu.roll(prev, d, axis=0), sh[:SUBLANES])
        sh = head if c == SUBLANES else jnp.concatenate([head, sh[SUBLANES:]], axis=0)
        acc = acc + sh * w[3 - d:4 - d]
    return _silu(acc)


def _conv_silu_sample(u, prev, w, b, pos):
    acc = b + u * w[3:4]
    for d in (1, 2, 3):
        sh = jnp.where(pos >= d, pltpu.roll(u, d, axis=0), pltpu.roll(prev, d + SAMPLE_LEN, axis=0))
        acc = acc + sh * w[3 - d:4 - d]
    return _silu(acc)


def _dot_t(a, b):
    return lax.dot_general(a, b, (((1,), (1,)), ((), ())), preferred_element_type=F32)


def _tdot(a, b):
    return lax.dot_general(a, b, (((0,), (0,)), ((), ())), preferred_element_type=F32)


def _rms(x, g):
    return x * lax.rsqrt(jnp.mean(x * x, axis=-1, keepdims=True) + EPS) * g


def _rmsnorm_body(x_ref, g_ref, o_ref):
    o_ref[...] = _rms(x_ref[...], g_ref[...]).astype(o_ref.dtype)


def _rmsnorm(x, g, out_dtype):
    t, d = x.shape
    row_spec = pl.BlockSpec((ROW_TILE, d), lambda i: (i, 0))
    return pl.pallas_call(
        _rmsnorm_body, grid=(t // ROW_TILE,),
        in_specs=[row_spec, pl.BlockSpec((1, d), lambda i: (0, 0))],
        out_specs=row_spec, out_shape=jax.ShapeDtypeStruct((t, d), out_dtype), name="rmsnorm",
    )(x, g.reshape(1, d))


def _matmul_body(be_ref, rc_ref, nu_ref, a_ref, *rest, n_w, tail_from, w_nk, has_res):
    del be_ref
    tail_ref = rest[0] if tail_from is not None else None
    rest = rest[1:] if tail_from is not None else rest
    w_refs, rest = rest[:n_w], rest[n_w:]
    res_ref = rest[0] if has_res else None
    rest = rest[1:] if has_res else rest
    o_ref, wb_refs = rest[0], rest[1:]
    b = pl.program_id(1)
    mm = _dot_t if w_nk else functools.partial(jnp.dot, preferred_element_type=F32)

    @pl.when(rc_ref[b] == 1)
    def _recast():
        for w_ref, wb_ref in zip(w_refs, wb_refs):
            wb_ref[...] = w_ref[...].astype(BF16)

    def compute(a):
        acc = mm(a, wb_refs[0][...])
        if n_w == 2:
            acc = _silu(acc) * mm(a, wb_refs[1][...])
        if has_res:
            acc = acc + res_ref[...]
        o_ref[...] = acc.astype(o_ref.dtype)

    if tail_from is None:
        pl.when(b < nu_ref[0])(lambda: compute(a_ref[...]))
    else:
        pl.when((b < nu_ref[0]) & (b < tail_from))(lambda: compute(a_ref[...]))
        pl.when((b < nu_ref[0]) & (b >= tail_from))(lambda: compute(tail_ref[...]))

    @pl.when(b >= nu_ref[0])
    def _unused():
        o_ref[...] = jnp.zeros(o_ref.shape, o_ref.dtype)


def _matmul(a, ws, *, tm, tn, n_cols, out_dtype, name, a_tail=None, block_e=None, recast=None, n_used=None,
            res=None, w_nk=False):
    k = a.shape[1]
    m = a.shape[0] + (0 if a_tail is None else a_tail.shape[0])
    nb, nj, n_w = m // tm, n_cols // tn, len(ws)
    tail_from = None if a_tail is None else a.shape[0] // tm
    if block_e is None:
        block_e = jnp.zeros((nb,), I32)
        recast = jnp.zeros((nb,), I32).at[0].set(1)
        n_used = jnp.full((1,), nb, I32)
    if a_tail is None:
        in_specs, args = [pl.BlockSpec((tm, k), lambda j, b, be, rc, nu: (b, 0))], [a]
    else:
        assert a.shape[0] % tm == 0 and a_tail.shape[0] % tm == 0
        in_specs = [pl.BlockSpec((tm, k), lambda j, b, be, rc, nu: (jnp.minimum(b, tail_from - 1), 0)),
                    pl.BlockSpec((tm, k), lambda j, b, be, rc, nu: (jnp.maximum(b - tail_from, 0), 0))]
        args = [a, a_tail]
    if w_nk:
        in_specs += [pl.BlockSpec((None, tn, k), lambda j, b, be, rc, nu: (be[b], j, 0))] * n_w
    else:
        in_specs += [pl.BlockSpec((None, k, tn), lambda j, b, be, rc, nu: (be[b], 0, j))] * n_w
    args += list(ws)
    if res is not None:
        in_specs.append(pl.BlockSpec((tm, tn), lambda j, b, be, rc, nu: (b, j)))
        args.append(res)
    out_bytes = jnp.dtype(out_dtype).itemsize
    vmem = (n_w * k * tn * (2 * 4 + 2) + (2 if a_tail is None else 4) * tm * k * 2 + 2 * tm * tn * out_bytes
            + (2 * tm * tn * 4 if res is not None else 0) + 2 * tm * tn * 4)
    grid_spec = pltpu.PrefetchScalarGridSpec(
        num_scalar_prefetch=3, grid=(nj, nb), in_specs=in_specs,
        out_specs=pl.BlockSpec((tm, tn), lambda j, b, be, rc, nu: (b, j)),
        scratch_shapes=[pltpu.VMEM((tn, k) if w_nk else (k, tn), BF16)] * n_w)
    body = functools.partial(_matmul_body, n_w=n_w, tail_from=tail_from, w_nk=w_nk, has_res=res is not None)
    return pl.pallas_call(
        body, grid_spec=grid_spec, out_shape=jax.ShapeDtypeStruct((m, n_cols), out_dtype), name=name,
        compiler_params=_vmem_params(vmem, ("arbitrary", "arbitrary")),
    )(block_e, recast, n_used, *args)


def _narrow_body(a_ref, w_ref, o_ref, *, n_valid):
    row = lax.broadcasted_iota(I32, w_ref.shape, 0)
    w = jnp.where(row < n_valid, w_ref[...], 0.0)
    o_ref[...] = _dot_t(a_ref[...], w.astype(BF16))


def _narrow_matmul(a, w_nk, col0, name):
    t, k = a.shape
    n_valid = w_nk.shape[1] - col0
    assert col0 % LANES == 0 and 0 < n_valid <= LANES
    return pl.pallas_call(
        functools.partial(_narrow_body, n_valid=n_valid), grid=(t // ROW_TILE,),
        in_specs=[pl.BlockSpec((ROW_TILE, k), lambda i: (i, 0)),
                  pl.BlockSpec((None, LANES, k), lambda i: (0, col0 // LANES, 0))],
        out_specs=pl.BlockSpec((ROW_TILE, LANES), lambda i: (i, 0)),
        out_shape=jax.ShapeDtypeStruct((t, LANES), F32), name=name,
    )(a, w_nk)


def _router_body(x_ref, ng_ref, w_ref, b_ref, e_ref, g_ref):
    xn = _rms(x_ref[...], ng_ref[...]).astype(BF16)
    logits = jnp.dot(xn, w_ref[...].astype(BF16), preferred_element_type=F32) + b_ref[...]
    lane = lax.broadcasted_iota(I32, logits.shape, 1)
    lane_f = lane.astype(F32)
    m1 = jnp.max(logits, axis=1, keepdims=True)
    i1 = jnp.min(jnp.where(logits == m1, lane_f, float(LANES)), axis=1, keepdims=True)
    rest = jnp.where(lane_f == i1, 2 * NEG, logits)
    m2 = jnp.max(rest, axis=1, keepdims=True)
    i2 = jnp.min(jnp.where(rest == m2, lane_f, float(LANES)), axis=1, keepdims=True)
    e2 = jnp.exp(m2 - m1)
    den = 1.0 + e2
    e_ref[...] = jnp.where(lane == 0, i1, jnp.where(lane == 1, i2, 0.0)).astype(I32)
    g_ref[...] = jnp.where(lane == 0, 1.0 / den, jnp.where(lane == 1, e2 / den, 0.0))


def _router(x, norm_g, w, b):
    t, k = x.shape
    wp = jnp.pad(w, ((0, 0), (0, LANES - N_EXPERTS)))
    bp = jnp.pad(b.reshape(1, N_EXPERTS), ((0, 0), (0, LANES - N_EXPERTS)), constant_values=NEG)
    e, g = pl.pallas_call(
        _router_body, grid=(t // ROW_TILE,),
        in_specs=[pl.BlockSpec((ROW_TILE, k), lambda i: (i, 0)), pl.BlockSpec((1, k), lambda i: (0, 0)),
                  pl.BlockSpec((k, LANES), lambda i: (0, 0)), pl.BlockSpec((1, LANES), lambda i: (0, 0))],
        out_specs=[pl.BlockSpec((ROW_TILE, LANES), lambda i: (i, 0))] * 2,
        out_shape=[jax.ShapeDtypeStruct((t, LANES), I32), jax.ShapeDtypeStruct((t, LANES), F32)],
        name="moe_router",
    )(x, norm_g.reshape(1, k), wp, bp)
    return e[:, :TOP_K], g


def _gather_body(tok_ref, src_ref, ng_ref, o_ref, buf_ref, sem_ref):
    b, nb = pl.program_id(0), pl.num_programs(0)
    rows = buf_ref.shape[1]

    def row_copy(blk, slot, r):
        return pltpu.make_async_copy(src_ref.at[pl.ds(tok_ref[blk * rows + r], 1)],
                                     buf_ref.at[slot, pl.ds(r, 1)], sem_ref.at[slot])

    def start_block(blk, slot):
        def body(r2, carry):
            for q in range(DMA_QUEUES):
                row_copy(blk, slot, r2 * DMA_QUEUES + q).start(priority=q)
            return carry
        lax.fori_loop(0, rows // DMA_QUEUES, body, 0, unroll=4)

    @pl.when(b == 0)
    def _first():
        start_block(0, 0)

    @pl.when(b + 1 < nb)
    def _next():
        start_block(b + 1, (b + 1) % 2)

    slot = b % 2

    def wait_body(r, carry):
        row_copy(b, slot, r).wait()
        return carry
    lax.fori_loop(0, rows, wait_body, 0, unroll=8)
    o_ref[...] = _rms(buf_ref[slot], ng_ref[...]).astype(o_ref.dtype)


def _gather_norm_rows(x, norm_g, slot_tok, rows):
    n_rows, d = slot_tok.shape[0], x.shape[1]
    grid_spec = pltpu.PrefetchScalarGridSpec(
        num_scalar_prefetch=1, grid=(n_rows // rows,),
        in_specs=[pl.BlockSpec(memory_space=pl.ANY), pl.BlockSpec((1, d), lambda b, tok: (0, 0))],
        out_specs=pl.BlockSpec((rows, d), lambda b, tok: (b, 0)),
        scratch_shapes=[pltpu.VMEM((2, rows, d), F32), pltpu.SemaphoreType.DMA((2,))])
    return pl.pallas_call(
        _gather_body, grid_spec=grid_spec, out_shape=jax.ShapeDtypeStruct((n_rows, d), BF16),
        compiler_params=_vmem_params(2 * rows * d * 4 + 4 * rows * d * 4, ("arbitrary",)), name="moe_gather",
    )(slot_tok, x, norm_g.reshape(1, d))


def _combine_body(pos_ref, yb_ref, x_ref, gate_ref, g_ref, op_ref, os_ref, buf_ref, sem_ref, *, prompt_tiles):
    i, n = pl.program_id(0), pl.num_programs(0)
    tt = x_ref.shape[0]

    def row_copy(tile, slot, tl, k):
        return pltpu.make_async_copy(yb_ref.at[pl.ds(pos_ref[(tile * tt + tl) * TOP_K + k], 1)],
                                     buf_ref.at[slot, pl.ds(k * tt + tl, 1)], sem_ref.at[slot])

    def start_tile(tile, slot):
        def body(tl, carry):
            for k in range(TOP_K):
                row_copy(tile, slot, tl, k).start(priority=k % DMA_QUEUES)
            return carry
        lax.fori_loop(0, tt, body, 0, unroll=4)

    @pl.when(i == 0)
    def _first():
        start_tile(0, 0)

    @pl.when(i + 1 < n)
    def _next():
        start_tile(i + 1, (i + 1) % 2)

    slot = i % 2

    def wait_body(tl, carry):
        for k in range(TOP_K):
            row_copy(i, slot, tl, k).wait()
        return carry
    lax.fori_loop(0, tt, wait_body, 0, unroll=4)

    gates = gate_ref[...]
    x = x_ref[...] + (gates[:, 0:1] * buf_ref[slot, :tt] + gates[:, 1:2] * buf_ref[slot, tt:])
    y = _rms(x, g_ref[...])

    @pl.when(i < prompt_tiles)
    def _prompt():
        op_ref[...] = y

    @pl.when(i >= prompt_tiles)
    def _sample():
        os_ref[...] = y


def _combine(yb, pos, gates, x, g, n_prompt):
    t, d = x.shape
    tt = COMBINE_ROWS
    pt = n_prompt // tt
    assert TOP_K == 2
    grid_spec = pltpu.PrefetchScalarGridSpec(
        num_scalar_prefetch=1, grid=(t // tt,),
        in_specs=[pl.BlockSpec(memory_space=pl.ANY), pl.BlockSpec((tt, d), lambda i, pos: (i, 0)),
                  pl.BlockSpec((tt, LANES), lambda i, pos: (i, 0)), pl.BlockSpec((1, d), lambda i, pos: (0, 0))],
        out_specs=[pl.BlockSpec((tt, d), lambda i, pos: (jnp.minimum(i, pt - 1), 0)),
                   pl.BlockSpec((tt, d), lambda i, pos: (jnp.maximum(i - pt, 0), 0))],
        scratch_shapes=[pltpu.VMEM((2, TOP_K * tt, d), F32), pltpu.SemaphoreType.DMA((2,))])
    return pl.pallas_call(
        functools.partial(_combine_body, prompt_tiles=pt), grid_spec=grid_spec,
        out_shape=[jax.ShapeDtypeStruct((n_prompt, d), F32), jax.ShapeDtypeStruct((t - n_prompt, d), F32)],
        compiler_params=_vmem_params((2 * TOP_K + 6) * tt * d * 4, ("arbitrary",)), name="moe_combine",
    )(pos, yb, x, gates, g.reshape(1, d))


def _split3(x):
    hi = x.astype(BF16)
    r1 = x - hi.astype(F32)
    mid = r1.astype(BF16)
    lo = (r1 - mid.astype(F32)).astype(BF16)
    return jnp.concatenate([hi, mid, lo], axis=1)


def _ssd_prompt_body(z_ref, xr_ref, br_ref, cr_ref, dt_ref, cwx_ref, cwb_ref, cwc_ref, cbx_ref, cbb_ref,
                     cbc_ref, dtb_ref, alog_ref, dsk_ref, ng_ref, e3_ref, y_ref, h_ref, px_ref, pb_ref, pc_ref,
                     cum_ref, cumt_ref, cumts_ref, cum3_ref, dt3_ref):
    c = xr_ref.shape[0]
    half = SSD_HEAD_DIM
    k, g = pl.program_id(1), pl.program_id(2)

    @pl.when(k == 0)
    def _init():
        h_ref[0, g] = jnp.zeros((SSD_GROUP_DIM, SSD_STATE), F32)
        px_ref[g] = jnp.zeros((SUBLANES, SSD_GROUP_DIM), F32)
        pb_ref[g] = jnp.zeros((SUBLANES, SSD_STATE), F32)
        pc_ref[g] = jnp.zeros((SUBLANES, SSD_STATE), F32)

    @pl.when(g == 0)
    def _per_chunk():
        dt = _softplus(dt_ref[...] + dtb_ref[...])
        cum = _cumsum_rows(dt * (-jnp.exp(alog_ref[...])))
        cum_t = cum.T
        cum_ref[...] = cum
        cumt_ref[...] = cum_t
        cumts_ref[...] = pltpu.roll(cum_t, half, axis=1)
        cum3_ref[...] = _split3(cum)
        dt3_ref[...] = _split3(dt)

    xr, br, cr = xr_ref[...], br_ref[...], cr_ref[...]
    xs = _conv_silu(xr, px_ref[g], cwx_ref[...], cbx_ref[...])
    bm = _conv_silu(br, pb_ref[g], cwb_ref[...], cbb_ref[...])
    cm = _conv_silu(cr, pc_ref[g], cwc_ref[...], cbc_ref[...])
    px_ref[g] = xr[c - SUBLANES:]
    pb_ref[g] = br[c - SUBLANES:]
    pc_ref[g] = cr[c - SUBLANES:]

    cum_e = jnp.dot(cum3_ref[...], e3_ref[0], preferred_element_type=F32)
    dt_e = jnp.dot(dt3_ref[...], e3_ref[0], preferred_element_type=F32)
    xd = xs * dt_e
    bm_bf, cm_bf = bm.astype(BF16), cm.astype(BF16)
    cb = _dot_t(cm_bf, bm_bf)

    lane = lax.broadcasted_iota(I32, (c, LANES), 1)
    trow = lax.broadcasted_iota(I32, (c, LANES), 0)
    low = lane < half
    s_lo = lane % half
    mask_a, mask_b = trow >= s_lo, trow >= s_lo + half
    cb_sw = pltpu.roll(cb, half, axis=1)
    cb_a, cb_b = jnp.where(low, cb, cb_sw), jnp.where(low, cb_sw, cb)
    low1 = low[0:1]

    ys = []
    for p in range(SSD_HEADS_PER_GROUP // 2):
        h0 = g * SSD_HEADS_PER_GROUP + 2 * p
        r0, r1 = cumt_ref[pl.ds(h0, 1), :], cumt_ref[pl.ds(h0 + 1, 1), :]
        r0s, r1s = cumts_ref[pl.ds(h0, 1), :], cumts_ref[pl.ds(h0 + 1, 1), :]
        row_a, row_b = jnp.where(low1, r0, r1s), jnp.where(low1, r0s, r1)
        cp = cum_e[:, p * LANES:(p + 1) * LANES]
        w_a = jnp.exp(jnp.where(mask_a, cp - row_a, NEG)) * cb_a
        w_b = jnp.exp(jnp.where(mask_b, cp - row_b, NEG)) * cb_b
        xp = xd[:, p * LANES:(p + 1) * LANES]
        top, bot = jnp.where(low, xp, 0.0), jnp.where(low, 0.0, xp)
        rhs = jnp.concatenate([top[:half], bot[:half], top[half:], bot[half:]], axis=0).astype(BF16)
        ys.append(jnp.dot(jnp.concatenate([w_a, w_b], axis=1).astype(BF16), rhs, preferred_element_type=F32))

    h_old = h_ref[0, g]
    y = jnp.concatenate(ys, axis=1) + _dot_t(cm_bf, h_old.astype(BF16)) * jnp.exp(cum_e) + dsk_ref[...] * xs
    y = y * _silu(z_ref[...])
    y = y * lax.rsqrt(jnp.mean(y * y, axis=1, keepdims=True) + EPS)
    y_ref[...] = (y * ng_ref[...]).astype(y_ref.dtype)

    tail_x = (xd * jnp.exp(cum_e[c - 1:c] - cum_e)).astype(BF16)
    upd = _tdot(tail_x, bm_bf)
    last = cum_ref[c - 1:c, :]
    for j in range(SSD_HEADS_PER_GROUP):
        rs = slice(j * half, (j + 1) * half)
        decay = jnp.exp(_lane_col(last, g * SSD_HEADS_PER_GROUP + j))
        h_ref[0, g, rs, :] = decay * h_old[rs] + upd[rs]


def _ssd_prompt(proj, dt_raw, p, n_seq, seq_len):
    c = SSD_CHUNK
    nc = seq_len // c
    gd, st = SSD_GROUP_DIM, SSD_STATE
    xo, bo, co = SSD_INNER // gd, 2 * SSD_INNER // st, (2 * SSD_INNER + SSD_GROUPS * st) // st
    cxo, cbo, cco = 0, SSD_INNER // st, (SSD_INNER + SSD_GROUPS * st) // st

    def rows(f):
        return lambda b, k, g: (b * nc + k, f(g))

    in_specs = [
        pl.BlockSpec((c, gd), rows(lambda g: g)),
        pl.BlockSpec((c, gd), rows(lambda g: xo + g)),
        pl.BlockSpec((c, st), rows(lambda g: bo + g)),
        pl.BlockSpec((c, st), rows(lambda g: co + g)),
        pl.BlockSpec((c, LANES), rows(lambda g: 0)),
        pl.BlockSpec((4, gd), lambda b, k, g: (0, cxo + g)),
        pl.BlockSpec((4, st), lambda b, k, g: (0, cbo + g)),
        pl.BlockSpec((4, st), lambda b, k, g: (0, cco + g)),
        pl.BlockSpec((1, gd), lambda b, k, g: (0, cxo + g)),
        pl.BlockSpec((1, st), lambda b, k, g: (0, cbo + g)),
        pl.BlockSpec((1, st), lambda b, k, g: (0, cco + g)),
        pl.BlockSpec((1, LANES), lambda b, k, g: (0, 0)),
        pl.BlockSpec((1, LANES), lambda b, k, g: (0, 0)),
        pl.BlockSpec((1, gd), lambda b, k, g: (0, g)),
        pl.BlockSpec((1, gd), lambda b, k, g: (0, g)),
        pl.BlockSpec((1, 3 * LANES, gd), lambda b, k, g: (g, 0, 0)),
    ]
    head = jnp.arange(LANES, dtype=I32)[None, :, None]
    chan = jnp.arange(gd, dtype=I32)[None, None, :]
    grp = jnp.arange(SSD_GROUPS, dtype=I32)[:, None, None]
    expand = (head == grp * SSD_HEADS_PER_GROUP + chan // SSD_HEAD_DIM).astype(BF16)
    expand3 = jnp.concatenate([expand] * 3, axis=1)
    assert c == LANES
    return pl.pallas_call(
        _ssd_prompt_body, grid=(n_seq, nc, SSD_GROUPS), in_specs=in_specs,
        out_specs=[pl.BlockSpec((c, gd), lambda b, k, g: (b * nc + k, g)),
                   pl.BlockSpec((1, SSD_GROUPS, gd, st), lambda b, k, g: (b, 0, 0, 0))],
        out_shape=[jax.ShapeDtypeStruct((n_seq * seq_len, SSD_INNER), BF16),
                   jax.ShapeDtypeStruct((n_seq, SSD_GROUPS, gd, st), F32)],
        scratch_shapes=[pltpu.VMEM((SSD_GROUPS, SUBLANES, gd), F32), pltpu.VMEM((SSD_GROUPS, SUBLANES, st), F32),
                        pltpu.VMEM((SSD_GROUPS, SUBLANES, st), F32),
                        pltpu.VMEM((c, LANES), F32), pltpu.VMEM((LANES, c), F32), pltpu.VMEM((LANES, c), F32),
                        pltpu.VMEM((c, 3 * LANES), BF16), pltpu.VMEM((c, 3 * LANES), BF16)],
        compiler_params=pltpu.CompilerParams(dimension_semantics=("arbitrary",) * 3),
        name="ssd_prompt",
    )(proj, proj, proj, proj, dt_raw, p["conv_w"], p["conv_w"], p["conv_w"], p["conv_b"], p["conv_b"],
      p["conv_b"], p["dt_bias"], p["a_log"], p["d_lanes"], p["norm"], expand3)


def _expand_heads(v, first_head, n_heads):
    lane = lax.broadcasted_iota(I32, (v.shape[0], LANES), 1)
    low = lane < SSD_HEAD_DIM
    out = []
    for q in range(n_heads // 2):
        j = first_head + 2 * q
        out.append(jnp.where(low, v[:, j:j + 1], v[:, j + 1:j + 2]))
    return jnp.concatenate(out, axis=1)


def _ssd_sample_body(z_ref, xr_ref, br_ref, cr_ref, dt_ref, prevx_ref, prevb_ref, prevc_ref, cwx_ref, cwb_ref,
                     cwc_ref, cbx_ref, cbb_ref, cbc_ref, dtb_ref, alog_ref, dsk_ref, ng_ref, h0_ref,
                     y_ref, h_ref):
    n = SUBLANES
    row1 = lax.broadcasted_iota(I32, (n, 1), 0)
    pos1, first1 = row1 % SAMPLE_LEN, row1 < SAMPLE_LEN

    def src(v, s):
        return jnp.where(first1, v[s:s + 1], v[SAMPLE_LEN + s:SAMPLE_LEN + s + 1])

    xs = _conv_silu_sample(xr_ref[...], prevx_ref[...], cwx_ref[...], cbx_ref[...], pos1)
    bm = _conv_silu_sample(br_ref[...], prevb_ref[...], cwb_ref[...], cbb_ref[...], pos1)
    cm = _conv_silu_sample(cr_ref[...], prevc_ref[...], cwc_ref[...], cbc_ref[...], pos1)
    dt = _softplus(dt_ref[...] + dtb_ref[...])
    la = dt * (-jnp.exp(alog_ref[...]))
    cum = la + jnp.where(pos1 >= 1, pltpu.roll(la, 1, axis=0), 0.0)
    cum = cum + jnp.where(pos1 >= 2, pltpu.roll(cum, 2, axis=0), 0.0)
    z = z_ref[...]
    pad_rows = LANES - n
    hrow = lax.broadcasted_iota(I32, (SSD_GROUP_DIM, SSD_STATE), 0) // SSD_HEAD_DIM

    ys = []
    for g in range(SSD_GROUPS):
        gs = slice(g * SSD_GROUP_DIM, (g + 1) * SSD_GROUP_DIM)
        ns = slice(g * SSD_STATE, (g + 1) * SSD_STATE)
        x_g, b_g, c_g = xs[:, gs], bm[:, ns], cm[:, ns]
        cum_e = _expand_heads(cum, g * SSD_HEADS_PER_GROUP, SSD_HEADS_PER_GROUP)
        dt_e = _expand_heads(dt, g * SSD_HEADS_PER_GROUP, SSD_HEADS_PER_GROUP)
        last_e = src(cum_e, SAMPLE_LEN - 1)
        y = jnp.zeros((n, SSD_GROUP_DIM), F32)
        for s in range(SAMPLE_LEN):
            cb_s = jnp.sum(c_g * src(b_g, s), axis=1, keepdims=True)
            decay = jnp.exp(jnp.where(pos1 >= s, cum_e - src(cum_e, s), NEG))
            y = y + decay * cb_s * src(dt_e, s) * src(x_g, s)
        c_bf = c_g.astype(BF16)
        h_old = [h0_ref[q, g] for q in range(SEQS_PER_STEP)]
        y_inter = jnp.where(first1, _dot_t(c_bf, h_old[0].astype(BF16)), _dot_t(c_bf, h_old[1].astype(BF16)))
        y = y + y_inter * jnp.exp(cum_e) + dsk_ref[:, gs] * x_g
        y = y * _silu(z[:, gs])
        y = y * lax.rsqrt(jnp.mean(y * y, axis=1, keepdims=True) + EPS)
        ys.append(y * ng_ref[:, gs])
        xw = x_g * jnp.exp(last_e - cum_e) * dt_e
        b_pad = jnp.concatenate([b_g, jnp.zeros((pad_rows, SSD_STATE), F32)], axis=0).astype(BF16)
        for q in range(SEQS_PER_STEP):
            own = (row1 >= q * SAMPLE_LEN) & (row1 < (q + 1) * SAMPLE_LEN)
            xw_pad = jnp.concatenate([jnp.where(own, xw, 0.0), jnp.zeros((pad_rows, SSD_GROUP_DIM), F32)],
                                     axis=0).astype(BF16)
            last_row = (q + 1) * SAMPLE_LEN - 1
            elast = jnp.exp(cum[last_row:last_row + 1])
            dec = jnp.zeros((SSD_GROUP_DIM, SSD_STATE), F32)
            for j in range(SSD_HEADS_PER_GROUP):
                hd = g * SSD_HEADS_PER_GROUP + j
                dec = jnp.where(hrow == j, elast[:, hd:hd + 1], dec)
            h_ref[q, g] = dec * h_old[q] + _tdot(xw_pad, b_pad)
    y_ref[...] = jnp.concatenate(ys, axis=1).astype(y_ref.dtype)


def _ssd_sample(proj, dt_raw, prev, h0, p, row0, n_seq):
    n = SUBLANES
    steps = n_seq // SEQS_PER_STEP
    r0 = row0 // n
    st = SSD_STATE
    zo, xo, bo, co = 0, 1, 2 * SSD_INNER // (SSD_GROUPS * st), (2 * SSD_INNER + SSD_GROUPS * st) // (SSD_GROUPS * st)
    gw = SSD_GROUPS * st
    in_specs = [
        pl.BlockSpec((n, SSD_INNER), lambda i: (r0 + i, zo)),
        pl.BlockSpec((n, SSD_INNER), lambda i: (r0 + i, xo)),
        pl.BlockSpec((n, gw), lambda i: (r0 + i, bo)),
        pl.BlockSpec((n, gw), lambda i: (r0 + i, co)),
        pl.BlockSpec((n, LANES), lambda i: (r0 + i, 0)),
        pl.BlockSpec((n, SSD_INNER), lambda i: (i, 0)),
        pl.BlockSpec((n, gw), lambda i: (i, SSD_INNER // gw)),
        pl.BlockSpec((n, gw), lambda i: (i, SSD_INNER // gw + 1)),
        pl.BlockSpec((4, SSD_INNER), lambda i: (0, 0)),
        pl.BlockSpec((4, gw), lambda i: (0, SSD_INNER // gw)),
        pl.BlockSpec((4, gw), lambda i: (0, SSD_INNER // gw + 1)),
        pl.BlockSpec((1, SSD_INNER), lambda i: (0, 0)),
        pl.BlockSpec((1, gw), lambda i: (0, SSD_INNER // gw)),
        pl.BlockSpec((1, gw), lambda i: (0, SSD_INNER // gw + 1)),
        pl.BlockSpec((1, LANES), lambda i: (0, 0)),
        pl.BlockSpec((1, LANES), lambda i: (0, 0)),
        pl.BlockSpec((1, SSD_INNER), lambda i: (0, 0)),
        pl.BlockSpec((1, SSD_INNER), lambda i: (0, 0)),
        pl.BlockSpec((SEQS_PER_STEP, SSD_GROUPS, SSD_GROUP_DIM, st), lambda i: (i, 0, 0, 0)),
    ]
    state_bytes = SEQS_PER_STEP * SSD_INNER * st * 4
    return pl.pallas_call(
        _ssd_sample_body, grid=(steps,), in_specs=in_specs,
        out_specs=[pl.BlockSpec((n, SSD_INNER), lambda i: (i, 0)),
                   pl.BlockSpec((SEQS_PER_STEP, SSD_GROUPS, SSD_GROUP_DIM, st), lambda i: (i, 0, 0, 0))],
        out_shape=[jax.ShapeDtypeStruct((n_seq * SAMPLE_LEN, SSD_INNER), BF16),
                   jax.ShapeDtypeStruct((n_seq, SSD_GROUPS, SSD_GROUP_DIM, st), F32)],
        compiler_params=_vmem_params(4 * state_bytes + (8 << 20), ("arbitrary",)),
        name="ssd_sample",
    )(proj, proj, proj, proj, dt_raw, prev, prev, prev, p["conv_w"], p["conv_w"], p["conv_w"], p["conv_b"],
      p["conv_b"], p["conv_b"], p["dt_bias"], p["a_log"], p["d_lanes"], p["norm"], h0)


def _ml_prompt_body(xc_ref, v_ref, o_ref, gt_ref, gb_ref, wq_ref, wk_ref, cw_ref, cb_ref, ng_ref,
                    y_ref, cs_ref, ns_ref, ms_ref, px_ref):
    c = xc_ref.shape[0]
    k, h = pl.program_id(1), pl.program_id(2)

    @pl.when(k == 0)
    def _init():
        cs_ref[0, h] = jnp.zeros((ML_V_DIM, ML_QK_DIM), F32)
        ns_ref[0, h] = jnp.zeros((1, ML_QK_DIM), F32)
        ms_ref[0, h] = jnp.zeros((1, LANES), F32)
        px_ref[h] = jnp.zeros((SUBLANES, ML_V_DIM), F32)

    u = xc_ref[...]
    xb = _conv_silu(u, px_ref[h], cw_ref[...], cb_ref[...]).astype(BF16)
    px_ref[h] = u[c - SUBLANES:]
    q = jnp.dot(xb, wq_ref[0], preferred_element_type=F32) * (ML_QK_DIM ** -0.5)
    kk = jnp.dot(xb, wk_ref[0], preferred_element_type=F32)
    q_bf, k_bf = q.astype(BF16), kk.astype(BF16)
    v = v_ref[...]

    gates = gt_ref[...] + gb_ref[...]
    lane = lax.broadcasted_iota(I32, gates.shape, 1)
    log_f = jnp.where((lane >= ML_HEADS) & (lane < 2 * ML_HEADS), -_softplus(-gates), 0.0)
    ig = _lane_col(gates, h)
    bcol = _lane_col(_cumsum_rows(log_f), ML_HEADS + h)

    row = lax.broadcasted_iota(I32, (c, c), 0)
    col = lax.broadcasted_iota(I32, (c, c), 1)
    eye, causal = row == col, col <= row
    logd = jnp.where(causal, bcol - _col_to_row(bcol, eye) + _col_to_row(ig, eye), NEG)
    m_prev = ms_ref[0, h][:, 0:1]
    inter = bcol + m_prev
    mt = jnp.maximum(jnp.max(logd, axis=1, keepdims=True), inter)
    s = _dot_t(q_bf, k_bf) * jnp.exp(logd - mt)
    wi = jnp.exp(inter - mt)
    c_old, n_old = cs_ref[0, h], ns_ref[0, h]
    num = jnp.dot(s.astype(BF16), v.astype(BF16), preferred_element_type=F32) + wi * _dot_t(q_bf, c_old.astype(BF16))
    den = jnp.sum(s, axis=1, keepdims=True) + wi * jnp.sum(q * n_old, axis=1, keepdims=True)
    hout = num / jnp.maximum(jnp.abs(den), jnp.exp(-mt))
    hout = hout * lax.rsqrt(jnp.mean(hout * hout, axis=1, keepdims=True) + EPS)
    y_ref[...] = (hout * ng_ref[...] * jax.nn.sigmoid(o_ref[...])).astype(y_ref.dtype)

    blast = bcol[c - 1:c]
    gcol = blast - bcol + ig
    carry = blast + m_prev
    m_new = jnp.maximum(carry, jnp.max(gcol, axis=0, keepdims=True))
    w_s, w_c = jnp.exp(gcol - m_new), jnp.exp(carry - m_new)
    cs_ref[0, h] = w_c * c_old + _tdot((v * w_s).astype(BF16), k_bf)
    ns_ref[0, h] = w_c * n_old + jnp.sum(w_s * kk, axis=0, keepdims=True)
    ms_ref[0, h] = jnp.broadcast_to(m_new, (1, LANES))


def _ml_prompt(proj, gates, p, n_seq, seq_len):
    c = ML_CHUNK
    nc = seq_len // c
    dv, dk = ML_V_DIM, ML_QK_DIM
    in_specs = [
        pl.BlockSpec((c, dv), lambda b, k, h: (b * nc + k, h)),
        pl.BlockSpec((c, dv), lambda b, k, h: (b * nc + k, ML_HEADS + h)),
        pl.BlockSpec((c, dv), lambda b, k, h: (b * nc + k, 2 * ML_HEADS + h)),
        pl.BlockSpec((c, LANES), lambda b, k, h: (b * nc + k, 0)),
        pl.BlockSpec((1, LANES), lambda b, k, h: (0, 0)),
        pl.BlockSpec((1, dv, dk), lambda b, k, h: (h, 0, 0)),
        pl.BlockSpec((1, dv, dk), lambda b, k, h: (h, 0, 0)),
        pl.BlockSpec((4, dv), lambda b, k, h: (0, h)),
        pl.BlockSpec((1, dv), lambda b, k, h: (0, h)),
        pl.BlockSpec((1, dv), lambda b, k, h: (0, h)),
    ]
    return pl.pallas_call(
        _ml_prompt_body, grid=(n_seq, nc, ML_HEADS), in_specs=in_specs,
        out_specs=[pl.BlockSpec((c, dv), lambda b, k, h: (b * nc + k, h)),
                   pl.BlockSpec((1, ML_HEADS, dv, dk), lambda b, k, h: (b, 0, 0, 0)),
                   pl.BlockSpec((1, ML_HEADS, 1, dk), lambda b, k, h: (b, 0, 0, 0)),
                   pl.BlockSpec((1, ML_HEADS, 1, LANES), lambda b, k, h: (b, 0, 0, 0))],
        out_shape=[jax.ShapeDtypeStruct((n_seq * seq_len, ML_INNER), BF16),
                   jax.ShapeDtypeStruct((n_seq, ML_HEADS, dv, dk), F32),
                   jax.ShapeDtypeStruct((n_seq, ML_HEADS, 1, dk), F32),
                   jax.ShapeDtypeStruct((n_seq, ML_HEADS, 1, LANES), F32)],
        scratch_shapes=[pltpu.VMEM((ML_HEADS, SUBLANES, dv), F32)],
        compiler_params=_vmem_params(4 * ML_HEADS * dv * dk * 4 + (8 << 20), ("arbitrary",) * 3),
        name="mlstm_prompt",
    )(proj, proj, proj, gates, p["gate_bias"], p["wq"], p["wk"], p["conv_w"], p["conv_b"], p["norm"])


def _ml_sample_body(xc_ref, v_ref, o_ref, gt_ref, prev_ref, gb_ref, wq_ref, wk_ref, cw_ref, cb_ref, ng_ref,
                    c0_ref, n0_ref, m0_ref, y_ref, cs_ref, ns_ref, ms_ref):
    n = SUBLANES
    row1 = lax.broadcasted_iota(I32, (n, 1), 0)
    pos1, first1 = row1 % SAMPLE_LEN, row1 < SAMPLE_LEN

    def src(v, s):
        return jnp.where(first1, v[s:s + 1], v[SAMPLE_LEN + s:SAMPLE_LEN + s + 1])

    def per_seq(vals):
        return jnp.where(first1, vals[0], vals[1])

    xs = _conv_silu_sample(xc_ref[...], prev_ref[...], cw_ref[...], cb_ref[...], pos1)
    gates = gt_ref[...] + gb_ref[...]
    lane = lax.broadcasted_iota(I32, gates.shape, 1)
    log_f = jnp.where((lane >= ML_HEADS) & (lane < 2 * ML_HEADS), -_softplus(-gates), 0.0)
    bsum = log_f + jnp.where(pos1 >= 1, pltpu.roll(log_f, 1, axis=0), 0.0)
    bsum = bsum + jnp.where(pos1 >= 2, pltpu.roll(bsum, 2, axis=0), 0.0)
    v_all, o_all = v_ref[...], o_ref[...]
    pad_rows = LANES - n

    ys = []
    for h in range(ML_HEADS):
        hs = slice(h * ML_V_DIM, (h + 1) * ML_V_DIM)
        xb = xs[:, hs].astype(BF16)
        q = jnp.dot(xb, wq_ref[h], preferred_element_type=F32) * (ML_QK_DIM ** -0.5)
        kk = jnp.dot(xb, wk_ref[h], preferred_element_type=F32)
        v = v_all[:, hs]
        ig = gates[:, h:h + 1]
        bcol = bsum[:, ML_HEADS + h:ML_HEADS + h + 1]
        c_old = [c0_ref[s_, h] for s_ in range(SEQS_PER_STEP)]
        n_old = [n0_ref[s_, h] for s_ in range(SEQS_PER_STEP)]
        m_old = [m0_ref[s_, h][:, 0:1] for s_ in range(SEQS_PER_STEP)]
        inter = bcol + per_seq(m_old)
        logd = [jnp.where(pos1 >= s, bcol - src(bcol, s) + src(ig, s), NEG) for s in range(SAMPLE_LEN)]
        mt = inter
        for ld in logd:
            mt = jnp.maximum(mt, ld)
        wi = jnp.exp(inter - mt)
        q_bf = q.astype(BF16)
        num = wi * per_seq([_dot_t(q_bf, cq.astype(BF16)) for cq in c_old])
        den = wi * jnp.sum(q * per_seq(n_old), axis=1, keepdims=True)
        for s in range(SAMPLE_LEN):
            sw = jnp.sum(q * src(kk, s), axis=1, keepdims=True) * jnp.exp(logd[s] - mt)
            num = num + sw * src(v, s)
            den = den + sw
        hout = num / jnp.maximum(jnp.abs(den), jnp.exp(-mt))
        hout = hout * lax.rsqrt(jnp.mean(hout * hout, axis=1, keepdims=True) + EPS)
        ys.append(hout * ng_ref[:, hs] * jax.nn.sigmoid(o_all[:, hs]))

        blast = src(bcol, SAMPLE_LEN - 1)
        gcol = blast - bcol + ig
        k_pad = jnp.concatenate([kk, jnp.zeros((pad_rows, ML_QK_DIM), F32)], axis=0).astype(BF16)
        for s_ in range(SEQS_PER_STEP):
            own = (row1 >= s_ * SAMPLE_LEN) & (row1 < (s_ + 1) * SAMPLE_LEN)
            last_row = (s_ + 1) * SAMPLE_LEN - 1
            carry = bcol[last_row:last_row + 1] + m_old[s_]
            m_new = jnp.maximum(carry, jnp.max(jnp.where(own, gcol, NEG), axis=0, keepdims=True))
            w_s = jnp.where(own, jnp.exp(gcol - m_new), 0.0)
            w_c = jnp.exp(carry - m_new)
            vw_pad = jnp.concatenate([v * w_s, jnp.zeros((pad_rows, ML_V_DIM), F32)], axis=0).astype(BF16)
            cs_ref[s_, h] = w_c * c_old[s_] + _tdot(vw_pad, k_pad)
            ns_ref[s_, h] = w_c * n_old[s_] + jnp.sum(w_s * kk, axis=0, keepdims=True)
            ms_ref[s_, h] = jnp.broadcast_to(m_new, (1, LANES))
    y_ref[...] = jnp.concatenate(ys, axis=1).astype(y_ref.dtype)


def _ml_sample(proj, gates, prev, c0, n0, m0, p, row0, n_seq):
    n = SUBLANES
    steps = n_seq // SEQS_PER_STEP
    r0 = row0 // n
    dv, dk = ML_V_DIM, ML_QK_DIM
    sq = SEQS_PER_STEP
    in_specs = [
        pl.BlockSpec((n, ML_INNER), lambda i: (r0 + i, 0)),
        pl.BlockSpec((n, ML_INNER), lambda i: (r0 + i, 1)),
        pl.BlockSpec((n, ML_INNER), lambda i: (r0 + i, 2)),
        pl.BlockSpec((n, LANES), lambda i: (r0 + i, 0)),
        pl.BlockSpec((n, ML_INNER), lambda i: (i, 0)),
        pl.BlockSpec((1, LANES), lambda i: (0, 0)),
        pl.BlockSpec((ML_HEADS, dv, dk), lambda i: (0, 0, 0)),
        pl.BlockSpec((ML_HEADS, dv, dk), lambda i: (0, 0, 0)),
        pl.BlockSpec((4, ML_INNER), lambda i: (0, 0)),
        pl.BlockSpec((1, ML_INNER), lambda i: (0, 0)),
        pl.BlockSpec((1, ML_INNER), lambda i: (0, 0)),
        pl.BlockSpec((sq, ML_HEADS, dv, dk), lambda i: (i, 0, 0, 0)),
        pl.BlockSpec((sq, ML_HEADS, 1, dk), lambda i: (i, 0, 0, 0)),
        pl.BlockSpec((sq, ML_HEADS, 1, LANES), lambda i: (i, 0, 0, 0)),
    ]
    state_bytes = sq * ML_HEADS * dv * dk * 4
    return pl.pallas_call(
        _ml_sample_body, grid=(steps,), in_specs=in_specs,
        out_specs=[pl.BlockSpec((n, ML_INNER), lambda i: (i, 0)),
                   pl.BlockSpec((sq, ML_HEADS, dv, dk), lambda i: (i, 0, 0, 0)),
                   pl.BlockSpec((sq, ML_HEADS, 1, dk), lambda i: (i, 0, 0, 0)),
                   pl.BlockSpec((sq, ML_HEADS, 1, LANES), lambda i: (i, 0, 0, 0))],
        out_shape=[jax.ShapeDtypeStruct((n_seq * SAMPLE_LEN, ML_INNER), BF16),
                   jax.ShapeDtypeStruct((n_seq, ML_HEADS, dv, dk), F32),
                   jax.ShapeDtypeStruct((n_seq, ML_HEADS, 1, dk), F32),
                   jax.ShapeDtypeStruct((n_seq, ML_HEADS, 1, LANES), F32)],
        compiler_params=_vmem_params(4 * state_bytes + (12 << 20), ("arbitrary",)),
        name="mlstm_sample",
    )(proj, proj, proj, gates, prev, p["gate_bias"], p["wq"], p["wk"], p["conv_w"], p["conv_b"], p["norm"],
      c0, n0, m0)


def _routing_tables(top_e, n_blocks):
    n_assign = top_e.size
    e_flat = top_e.reshape(-1)
    onehot = (e_flat[:, None] == jnp.arange(N_EXPERTS, dtype=I32)[None, :]).astype(I32)
    csum = jnp.cumsum(onehot, axis=0)
    rank = jnp.sum((csum - onehot) * onehot, axis=1)
    counts = csum[-1]
    padded = (counts + MOE_ROWS - 1) // MOE_ROWS * MOE_ROWS
    pends = jnp.cumsum(padded)
    pos = jnp.sum(onehot * (pends - padded)[None, :], axis=1) + rank
    n_rows = n_blocks * MOE_ROWS
    slot_tok = jnp.zeros((n_rows,), I32).at[pos].set(jnp.arange(n_assign, dtype=I32) // TOP_K)
    block_start = jnp.arange(n_blocks, dtype=I32) * MOE_ROWS
    block_e = jnp.minimum(jnp.sum((pends[None, :] <= block_start[:, None]).astype(I32), axis=1), N_EXPERTS - 1)
    recast = jnp.concatenate([jnp.ones((1,), I32), (block_e[1:] != block_e[:-1]).astype(I32)])
    n_used = (pends[-1:] // MOE_ROWS).astype(I32)
    return pos, slot_tok, block_e, recast, n_used


def kernel(x_prompt, x_sample, state_ssm, state_ssm_conv, state_mlstm_c, state_mlstm_n, state_mlstm_m, state_mlstm_conv, norm_mix_a, ssd_w_in, ssd_conv_w, ssd_conv_b, ssd_dt_bias, ssd_a_log, ssd_d, ssd_norm, ssd_w_out, norm_ffn_a, ffn_w_gate, ffn_w_up, ffn_w_down, norm_mix_b, ml_w_in, ml_conv_w, ml_conv_b, ml_w_q, ml_w_k, ml_b_i, ml_b_f, ml_norm, ml_w_out, norm_ffn_b, moe_w_router, moe_b_router, moe_w_gate, moe_w_up, moe_w_down, final_norm):
    bp, lp, d = x_prompt.shape
    bs, ls, _ = x_sample.shape
    assert ls == SAMPLE_LEN and lp % ML_CHUNK == 0 and lp % SSD_CHUNK == 0 and bs % SEQS_PER_STEP == 0
    tp, ts = bp * lp, bs * ls
    t = tp + ts
    assert tp % ROW_TILE == 0 and ts % ROW_TILE == 0
    tall = t // 8
    assert t % (8 * 32) == 0
    x = jnp.concatenate([x_prompt.reshape(tp, d), x_sample.reshape(ts, d)], axis=0)

    def pad_lanes(v):
        return jnp.pad(v.reshape(1, -1), ((0, 0), (0, LANES - v.size)))

    def prev_rows(state):
        return jnp.pad(state, ((0, 0), (1, 0), (0, 0))).reshape(state.shape[0] * SAMPLE_LEN, state.shape[2])

    z_cols = SSD_INNER + SSD_XBC
    def last_rows(proj, lo, hi):
        rows_p = jnp.stack([proj[(b + 1) * lp - 3:(b + 1) * lp, lo:hi] for b in range(bp)])
        rows_s = proj[tp:, lo:hi].reshape(bs, ls, hi - lo)[:, ls - 3:]
        return rows_p, rows_s

    xn = _rmsnorm(x, norm_mix_a[0], BF16)
    ssd_w_in_nk = jnp.swapaxes(ssd_w_in, 1, 2)
    proj = _matmul(xn, [ssd_w_in_nk], tm=tall, tn=1024, n_cols=z_cols, out_dtype=F32, name="ssd_in", w_nk=True)
    dt_raw = _narrow_matmul(xn, ssd_w_in_nk, z_cols, "ssd_dt")
    ssd_p = dict(conv_w=ssd_conv_w[0], conv_b=ssd_conv_b[0].reshape(1, -1), dt_bias=pad_lanes(ssd_dt_bias[0]),
                 a_log=pad_lanes(ssd_a_log[0]), d_lanes=jnp.repeat(ssd_d[0], SSD_HEAD_DIM).reshape(1, -1),
                 norm=ssd_norm[0].reshape(1, -1))
    y_p, ssm_p = _ssd_prompt(proj, dt_raw, ssd_p, bp, lp)
    y_s, ssm_s = _ssd_sample(proj, dt_raw, prev_rows(state_ssm_conv[0]),
                             state_ssm[0].reshape(bs, SSD_GROUPS, SSD_GROUP_DIM, SSD_STATE), ssd_p, tp, bs)
    ssm_conv_p, ssm_conv_s = last_rows(proj, SSD_INNER, z_cols)
    x = _matmul(y_p, [ssd_w_out], a_tail=y_s, tm=ROW_TILE, tn=512, n_cols=d, out_dtype=F32, name="ssd_out", res=x)

    xn = _rmsnorm(x, norm_ffn_a[0], BF16)
    hid = _matmul(xn, [ffn_w_gate, ffn_w_up], tm=tall, tn=512, n_cols=ffn_w_gate.shape[2],
                  out_dtype=BF16, name="ffn_gate_up")
    x = _matmul(hid, [ffn_w_down], tm=tall // 2, tn=512, n_cols=d, out_dtype=F32, name="ffn_down", res=x)

    xn = _rmsnorm(x, norm_mix_b[0], BF16)
    ml_cols = 3 * ML_INNER
    ml_w_in_nk = jnp.swapaxes(ml_w_in, 1, 2)
    proj2 = _matmul(xn, [ml_w_in_nk], tm=tall, tn=1024, n_cols=ml_cols, out_dtype=F32, name="ml_in", w_nk=True)
    gates = _narrow_matmul(xn, ml_w_in_nk, ml_cols, "ml_gates")
    ml_p = dict(conv_w=ml_conv_w[0], conv_b=ml_conv_b[0].reshape(1, -1),
                gate_bias=pad_lanes(jnp.concatenate([ml_b_i[0], ml_b_f[0]])),
                wq=ml_w_q[0].astype(BF16), wk=ml_w_k[0].astype(BF16), norm=ml_norm[0].reshape(1, -1))
    y_p, c_p, n_p, m_p = _ml_prompt(proj2, gates, ml_p, bp, lp)
    y_s, c_s, n_s, m_s = _ml_sample(
        proj2, gates, prev_rows(state_mlstm_conv[0]), state_mlstm_c[0],
        state_mlstm_n[0].reshape(bs, ML_HEADS, 1, ML_QK_DIM),
        jnp.broadcast_to(state_mlstm_m[0][:, :, None, None], (bs, ML_HEADS, 1, LANES)), ml_p, tp, bs)
    ml_conv_p, ml_conv_s = last_rows(proj2, 0, ML_INNER)
    x = _matmul(y_p, [ml_w_out], a_tail=y_s, tm=ROW_TILE, tn=512, n_cols=d, out_dtype=F32, name="ml_out", res=x)

    top_e, top_g = _router(x, norm_ffn_b[0], moe_w_router[0], moe_b_router[0])
    n_blocks = -(-(t * TOP_K + N_EXPERTS * (MOE_ROWS - 1)) // MOE_ROWS)
    pos, slot_tok, block_e, recast, n_used = _routing_tables(top_e, n_blocks)
    xg = _gather_norm_rows(x, norm_ffn_b[0], slot_tok, MOE_ROWS)
    route = dict(block_e=block_e, recast=recast, n_used=n_used)
    hid = _matmul(xg, [moe_w_gate[0], moe_w_up[0]], tm=MOE_ROWS, tn=1024, n_cols=moe_w_gate.shape[3],
                  out_dtype=BF16, name="moe_gate_up", **route)
    yb = _matmul(hid, [moe_w_down[0]], tm=MOE_ROWS, tn=512, n_cols=d, out_dtype=F32, name="moe_down", **route)
    out_p, out_s = _combine(yb, pos, top_g, x, final_norm, tp)

    return (out_p.reshape(bp, lp, d), out_s.reshape(bs, ls, d),
            ssm_p.reshape(1, bp, *state_ssm.shape[2:]), ssm_conv_p[None],
            c_p[None], n_p.reshape(1, bp, ML_HEADS, ML_QK_DIM), m_p[:, :, 0, 0][None], ml_conv_p[None],
            ssm_s.reshape(1, bs, *state_ssm.shape[2:]), ssm_conv_s[None],
            c_s[None], n_s.reshape(1, bs, ML_HEADS, ML_QK_DIM), m_s[:, :, 0, 0][None], ml_conv_s[None])
```

```python
import functools

import jax
import jax.numpy as jnp
from jax import lax
from jax.experimental import pallas as pl
from jax.experimental.pallas import tpu as pltpu

F32, BF16, I32 = jnp.float32, jnp.bfloat16, jnp.int32

EPS = 1e-6
D_MODEL = 2048
SSD_INNER = 4096
SSD_HEAD_DIM = 64
SSD_GROUPS = 8
SSD_STATE = 128
SSD_GROUP_DIM = SSD_INNER // SSD_GROUPS
SSD_HEADS_PER_GROUP = SSD_GROUP_DIM // SSD_HEAD_DIM
SSD_XBC = SSD_INNER + 2 * SSD_GROUPS * SSD_STATE
ML_HEADS = 8
ML_INNER = 4096
ML_V_DIM = 512
ML_QK_DIM = 256
N_EXPERTS = 8
TOP_K = 2

LANES = 128
SUBLANES = 8
SSD_CHUNK = 128
ML_CHUNK = 256
SAMPLE_LEN = 4
SEQS_PER_STEP = SUBLANES // SAMPLE_LEN
ROW_TILE = 512
MOE_ROWS = 512
COMBINE_ROWS = 256
NEG = -1e30
VMEM_CAP = 60 << 20


def _vmem_params(nbytes, semantics):
    return pltpu.CompilerParams(dimension_semantics=semantics,
                                vmem_limit_bytes=int(min(nbytes + (8 << 20), VMEM_CAP)))


def _softplus(x):
    return jnp.maximum(x, 0.0) + jnp.log1p(jnp.exp(-jnp.abs(x)))


def _silu(x):
    return x * jax.nn.sigmoid(x)


def _cumsum_rows(x):
    n = x.shape[0]
    row = lax.broadcasted_iota(I32, x.shape, 0)
    s = 1
    while s < n:
        x = x + jnp.where(row >= s, pltpu.roll(x, s, axis=0), 0.0)
        s *= 2
    return x


def _lane_col(x, lane_idx):
    lane = lax.broadcasted_iota(I32, x.shape, 1)
    return jnp.sum(jnp.where(lane == lane_idx, x, 0.0), axis=1, keepdims=True)


def _col_to_row(col, eye):
    return jnp.sum(jnp.where(eye, col, 0.0), axis=0, keepdims=True)


def _conv_silu(u, prev, w, b):
    c, width = u.shape
    row8 = lax.broadcasted_iota(I32, (SUBLANES, width), 0)
    acc = b + u * w[3:4]
    for d in (1, 2, 3):
        sh = pltpu.roll(u, d, axis=0)
        head = jnp.where(row8 < d, pltpu.roll(prev, d, axis=0), sh[:SUBLANES])
        sh = head if c == SUBLANES else jnp.concatenate([head, sh[SUBLANES:]], axis=0)
        acc = acc + sh * w[3 - d:4 - d]
    return _silu(acc)


def _conv_silu_sample(u, prev, w, b, pos):
    acc = b + u * w[3:4]
    for d in (1, 2, 3):
        sh = jnp.where(pos >= d, pltpu.roll(u, d, axis=0), pltpu.roll(prev, d + SAMPLE_LEN, axis=0))
        acc = acc + sh * w[3 - d:4 - d]
    return _silu(acc)


def _dot_t(a, b):
    return lax.dot_general(a, b, (((1,), (1,)), ((), ())), preferred_element_type=F32)


def _tdot(a, b):
    return lax.dot_general(a, b, (((0,), (0,)), ((), ())), preferred_element_type=F32)


def _rms(x, g):
    return x * lax.rsqrt(jnp.mean(x * x, axis=-1, keepdims=True) + EPS) * g


def _rmsnorm_body(x_ref, g_ref, o_ref):
    o_ref[...] = _rms(x_ref[...], g_ref[...]).astype(o_ref.dtype)


def _rmsnorm(x, g, out_dtype):
    t, d = x.shape
    row_spec = pl.BlockSpec((ROW_TILE, d), lambda i: (i, 0))
    return pl.pallas_call(
        _rmsnorm_body, grid=(t // ROW_TILE,),
        in_specs=[row_spec, pl.BlockSpec((1, d), lambda i: (0, 0))],
        out_specs=row_spec, out_shape=jax.ShapeDtypeStruct((t, d), out_dtype), name="rmsnorm",
    )(x, g.reshape(1, d))


def _matmul_body(be_ref, rc_ref, nu_ref, a_ref, *rest, n_w, tail_from, w_nk, has_res):
    del be_ref
    tail_ref = rest[0] if tail_from is not None else None
    rest = rest[1:] if tail_from is not None else rest
    w_refs, rest = rest[:n_w], rest[n_w:]
    res_ref = rest[0] if has_res else None
    rest = rest[1:] if has_res else rest
    o_ref, wb_refs = rest[0], rest[1:]
    b = pl.program_id(1)
    mm = _dot_t if w_nk else functools.partial(jnp.dot, preferred_element_type=F32)

    @pl.when(rc_ref[b] == 1)
    def _recast():
        for w_ref, wb_ref in zip(w_refs, wb_refs):
            wb_ref[...] = w_ref[...].astype(BF16)

    def compute(a):
        acc = mm(a, wb_refs[0][...])
        if n_w == 2:
            acc = _silu(acc) * mm(a, wb_refs[1][...])
        if has_res:
            acc = acc + res_ref[...]
        o_ref[...] = acc.astype(o_ref.dtype)

    if tail_from is None:
        pl.when(b < nu_ref[0])(lambda: compute(a_ref[...]))
    else:
        pl.when((b < nu_ref[0]) & (b < tail_from))(lambda: compute(a_ref[...]))
        pl.when((b < nu_ref[0]) & (b >= tail_from))(lambda: compute(tail_ref[...]))

    @pl.when(b >= nu_ref[0])
    def _unused():
        o_ref[...] = jnp.zeros(o_ref.shape, o_ref.dtype)


def _matmul(a, ws, *, tm, tn, n_cols, out_dtype, name, a_tail=None, block_e=None, recast=None, n_used=None,
            res=None, w_nk=False):
    k = a.shape[1]
    m = a.shape[0] + (0 if a_tail is None else a_tail.shape[0])
    nb, nj, n_w = m // tm, n_cols // tn, len(ws)
    tail_from = None if a_tail is None else a.shape[0] // tm
    if block_e is None:
        block_e = jnp.zeros((nb,), I32)
        recast = jnp.zeros((nb,), I32).at[0].set(1)
        n_used = jnp.full((1,), nb, I32)
    if a_tail is None:
        in_specs, args = [pl.BlockSpec((tm, k), lambda j, b, be, rc, nu: (b, 0))], [a]
    else:
        assert a.shape[0] % tm == 0 and a_tail.shape[0] % tm == 0
        in_specs = [pl.BlockSpec((tm, k), lambda j, b, be, rc, nu: (jnp.minimum(b, tail_from - 1), 0)),
                    pl.BlockSpec((tm, k), lambda j, b, be, rc, nu: (jnp.maximum(b - tail_from, 0), 0))]
        args = [a, a_tail]
    if w_nk:
        in_specs += [pl.BlockSpec((None, tn, k), lambda j, b, be, rc, nu: (be[b], j, 0))] * n_w
    else:
        in_specs += [pl.BlockSpec((None, k, tn), lambda j, b, be, rc, nu: (be[b], 0, j))] * n_w
    args += list(ws)
    if res is not None:
        in_specs.append(pl.BlockSpec((tm, tn), lambda j, b, be, rc, nu: (b, j)))
        args.append(res)
    out_bytes = jnp.dtype(out_dtype).itemsize
    vmem = (n_w * k * tn * (2 * 4 + 2) + (2 if a_tail is None else 4) * tm * k * 2 + 2 * tm * tn * out_bytes
            + (2 * tm * tn * 4 if res is not None else 0) + 2 * tm * tn * 4)
    grid_spec = pltpu.PrefetchScalarGridSpec(
        num_scalar_prefetch=3, grid=(nj, nb), in_specs=in_specs,
        out_specs=pl.BlockSpec((tm, tn), lambda j, b, be, rc, nu: (b, j)),
        scratch_shapes=[pltpu.VMEM((tn, k) if w_nk else (k, tn), BF16)] * n_w)
    body = functools.partial(_matmul_body, n_w=n_w, tail_from=tail_from, w_nk=w_nk, has_res=res is not None)
    return pl.pallas_call(
        body, grid_spec=grid_spec, out_shape=jax.ShapeDtypeStruct((m, n_cols), out_dtype), name=name,
        compiler_params=_vmem_params(vmem, ("arbitrary", "arbitrary")),
    )(block_e, recast, n_used, *args)


def _norm_narrow_body(x_ref, g_ref, w_ref, xn_ref, o_ref, *, n_valid):
    xn = _rms(x_ref[...], g_ref[...]).astype(BF16)
    xn_ref[...] = xn
    row = lax.broadcasted_iota(I32, w_ref.shape, 0)
    w = jnp.where(row < n_valid, w_ref[...], 0.0)
    o_ref[...] = _dot_t(xn, w.astype(BF16))


def _rmsnorm_narrow_matmul(x, g, w_nk, col0, name):
    t, k = x.shape
    n_valid = w_nk.shape[1] - col0
    assert col0 % LANES == 0 and 0 < n_valid <= LANES
    row_spec = pl.BlockSpec((ROW_TILE, k), lambda i: (i, 0))
    return pl.pallas_call(
        functools.partial(_norm_narrow_body, n_valid=n_valid), grid=(t // ROW_TILE,),
        in_specs=[row_spec, pl.BlockSpec((1, k), lambda i: (0, 0)),
                  pl.BlockSpec((None, LANES, k), lambda i: (0, col0 // LANES, 0))],
        out_specs=[row_spec, pl.BlockSpec((ROW_TILE, LANES), lambda i: (i, 0))],
        out_shape=[jax.ShapeDtypeStruct((t, k), BF16), jax.ShapeDtypeStruct((t, LANES), F32)], name=name,
    )(x, g.reshape(1, k), w_nk)


def _router_body(x_ref, ng_ref, w_ref, b_ref, e_ref, g_ref):
    xn = _rms(x_ref[...], ng_ref[...]).astype(BF16)
    logits = jnp.dot(xn, w_ref[...].astype(BF16), preferred_element_type=F32) + b_ref[...]
    lane = lax.broadcasted_iota(I32, logits.shape, 1)
    lane_f = lane.astype(F32)
    m1 = jnp.max(logits, axis=1, keepdims=True)
    i1 = jnp.min(jnp.where(logits == m1, lane_f, float(LANES)), axis=1, keepdims=True)
    rest = jnp.where(lane_f == i1, 2 * NEG, logits)
    m2 = jnp.max(rest, axis=1, keepdims=True)
    i2 = jnp.min(jnp.where(rest == m2, lane_f, float(LANES)), axis=1, keepdims=True)
    e2 = jnp.exp(m2 - m1)
    den = 1.0 + e2
    e_ref[...] = jnp.where(lane == 0, i1, jnp.where(lane == 1, i2, 0.0)).astype(I32)
    g_ref[...] = jnp.where(lane == 0, 1.0 / den, jnp.where(lane == 1, e2 / den, 0.0))


def _router(x, norm_g, w, b):
    t, k = x.shape
    wp = jnp.pad(w, ((0, 0), (0, LANES - N_EXPERTS)))
    bp = jnp.pad(b.reshape(1, N_EXPERTS), ((0, 0), (0, LANES - N_EXPERTS)), constant_values=NEG)
    e, g = pl.pallas_call(
        _router_body, grid=(t // ROW_TILE,),
        in_specs=[pl.BlockSpec((ROW_TILE, k), lambda i: (i, 0)), pl.BlockSpec((1, k), lambda i: (0, 0)),
                  pl.BlockSpec((k, LANES), lambda i: (0, 0)), pl.BlockSpec((1, LANES), lambda i: (0, 0))],
        out_specs=[pl.BlockSpec((ROW_TILE, LANES), lambda i: (i, 0))] * 2,
        out_shape=[jax.ShapeDtypeStruct((t, LANES), I32), jax.ShapeDtypeStruct((t, LANES), F32)],
        name="moe_router",
    )(x, norm_g.reshape(1, k), wp, bp)
    return e[:, :TOP_K], g


def _gather_body(tok_ref, nu_ref, src_ref, ng_ref, o_ref, buf_ref, sem_ref):
    b, nb = pl.program_id(0), nu_ref[0]
    rows = buf_ref.shape[1]

    def row_copy(blk, slot, r):
        return pltpu.make_async_copy(src_ref.at[pl.ds(tok_ref[blk * rows + r], 1)],
                                     buf_ref.at[slot, pl.ds(r, 1)], sem_ref.at[slot])

    def start_block(blk, slot):
        def body(r, carry):
            row_copy(blk, slot, r).start()
            return carry
        lax.fori_loop(0, rows, body, 0, unroll=8)

    @pl.when(b == 0)
    def _first():
        start_block(0, 0)

    @pl.when(b + 1 < nb)
    def _next():
        start_block(b + 1, (b + 1) % 2)

    slot = b % 2

    @pl.when(b < nb)
    def _used():
        def wait_body(r, carry):
            row_copy(b, slot, r).wait()
            return carry
        lax.fori_loop(0, rows, wait_body, 0, unroll=8)
        o_ref[...] = _rms(buf_ref[slot], ng_ref[...]).astype(o_ref.dtype)

    @pl.when(b >= nb)
    def _unused():
        o_ref[...] = jnp.zeros(o_ref.shape, o_ref.dtype)


def _gather_norm_rows(x, norm_g, slot_tok, n_used, rows):
    n_rows, d = slot_tok.shape[0], x.shape[1]
    grid_spec = pltpu.PrefetchScalarGridSpec(
        num_scalar_prefetch=2, grid=(n_rows // rows,),
        in_specs=[pl.BlockSpec(memory_space=pl.ANY), pl.BlockSpec((1, d), lambda b, tok, nu: (0, 0))],
        out_specs=pl.BlockSpec((rows, d), lambda b, tok, nu: (b, 0)),
        scratch_shapes=[pltpu.VMEM((2, rows, d), F32), pltpu.SemaphoreType.DMA((2,))])
    return pl.pallas_call(
        _gather_body, grid_spec=grid_spec, out_shape=jax.ShapeDtypeStruct((n_rows, d), BF16),
        compiler_params=_vmem_params(2 * rows * d * 4 + 4 * rows * d * 4, ("arbitrary",)), name="moe_gather",
    )(slot_tok, n_used, x, norm_g.reshape(1, d))


def _combine_body(pos_ref, yb_ref, x_ref, gate_ref, g_ref, op_ref, os_ref, buf_ref, sem_ref, *, prompt_tiles):
    i, n = pl.program_id(0), pl.num_programs(0)
    tt = x_ref.shape[0]

    def row_copy(tile, slot, tl, k):
        return pltpu.make_async_copy(yb_ref.at[pl.ds(pos_ref[(tile * tt + tl) * TOP_K + k], 1)],
                                     buf_ref.at[slot, pl.ds(k * tt + tl, 1)], sem_ref.at[slot])

    def start_tile(tile, slot):
        def body(tl, carry):
            for k in range(TOP_K):
                row_copy(tile, slot, tl, k).start()
            return carry
        lax.fori_loop(0, tt, body, 0, unroll=4)

    @pl.when(i == 0)
    def _first():
        start_tile(0, 0)

    @pl.when(i + 1 < n)
    def _next():
        start_tile(i + 1, (i + 1) % 2)

    slot = i % 2

    def wait_body(tl, carry):
        for k in range(TOP_K):
            row_copy(i, slot, tl, k).wait()
        return carry
    lax.fori_loop(0, tt, wait_body, 0, unroll=4)

    gates = gate_ref[...]
    x = x_ref[...] + (gates[:, 0:1] * buf_ref[slot, :tt] + gates[:, 1:2] * buf_ref[slot, tt:])
    y = _rms(x, g_ref[...])

    @pl.when(i < prompt_tiles)
    def _prompt():
        op_ref[...] = y

    @pl.when(i >= prompt_tiles)
    def _sample():
        os_ref[...] = y


def _combine(yb, pos, gates, x, g, n_prompt):
    t, d = x.shape
    tt = COMBINE_ROWS
    pt = n_prompt // tt
    assert TOP_K == 2
    grid_spec = pltpu.PrefetchScalarGridSpec(
        num_scalar_prefetch=1, grid=(t // tt,),
        in_specs=[pl.BlockSpec(memory_space=pl.ANY), pl.BlockSpec((tt, d), lambda i, pos: (i, 0)),
                  pl.BlockSpec((tt, LANES), lambda i, pos: (i, 0)), pl.BlockSpec((1, d), lambda i, pos: (0, 0))],
        out_specs=[pl.BlockSpec((tt, d), lambda i, pos: (jnp.minimum(i, pt - 1), 0)),
                   pl.BlockSpec((tt, d), lambda i, pos: (jnp.maximum(i - pt, 0), 0))],
        scratch_shapes=[pltpu.VMEM((2, TOP_K * tt, d), F32), pltpu.SemaphoreType.DMA((2,))])
    return pl.pallas_call(
        functools.partial(_combine_body, prompt_tiles=pt), grid_spec=grid_spec,
        out_shape=[jax.ShapeDtypeStruct((n_prompt, d), F32), jax.ShapeDtypeStruct((t - n_prompt, d), F32)],
        compiler_params=_vmem_params((2 * TOP_K + 6) * tt * d * 4, ("arbitrary",)), name="moe_combine",
    )(pos, yb, x, gates, g.reshape(1, d))


def _split3(x):
    hi = x.astype(BF16)
    r1 = x - hi.astype(F32)
    mid = r1.astype(BF16)
    lo = (r1 - mid.astype(F32)).astype(BF16)
    return jnp.concatenate([hi, mid, lo], axis=1)


def _ssd_prompt_body(z_ref, xr_ref, br_ref, cr_ref, dt_ref, cwx_ref, cwb_ref, cwc_ref, cbx_ref, cbb_ref,
                     cbc_ref, dtb_ref, alog_ref, dsk_ref, ng_ref, e3_ref, y_ref, h_ref, px_ref, pb_ref, pc_ref,
                     cum_ref, cumt_ref, cumts_ref, cum3_ref, dt3_ref):
    c = xr_ref.shape[0]
    half = SSD_HEAD_DIM
    k, g = pl.program_id(1), pl.program_id(2)

    @pl.when(k == 0)
    def _init():
        h_ref[0, g] = jnp.zeros((SSD_GROUP_DIM, SSD_STATE), F32)
        px_ref[g] = jnp.zeros((SUBLANES, SSD_GROUP_DIM), F32)
        pb_ref[g] = jnp.zeros((SUBLANES, SSD_STATE), F32)
        pc_ref[g] = jnp.zeros((SUBLANES, SSD_STATE), F32)

    @pl.when(g == 0)
    def _per_chunk():
        dt = _softplus(dt_ref[...] + dtb_ref[...])
        cum = _cumsum_rows(dt * (-jnp.exp(alog_ref[...])))
        cum_t = cum.T
        cum_ref[...] = cum
        cumt_ref[...] = cum_t
        cumts_ref[...] = pltpu.roll(cum_t, half, axis=1)
        cum3_ref[...] = _split3(cum)
        dt3_ref[...] = _split3(dt)

    xr, br, cr = xr_ref[...], br_ref[...], cr_ref[...]
    xs = _conv_silu(xr, px_ref[g], cwx_ref[...], cbx_ref[...])
    bm = _conv_silu(br, pb_ref[g], cwb_ref[...], cbb_ref[...])
    cm = _conv_silu(cr, pc_ref[g], cwc_ref[...], cbc_ref[...])
    px_ref[g] = xr[c - SUBLANES:]
    pb_ref[g] = br[c - SUBLANES:]
    pc_ref[g] = cr[c - SUBLANES:]

    cum_e = jnp.dot(cum3_ref[...], e3_ref[0], preferred_element_type=F32)
    dt_e = jnp.dot(dt3_ref[...], e3_ref[0], preferred_element_type=F32)
    xd = xs * dt_e
    bm_bf, cm_bf = bm.astype(BF16), cm.astype(BF16)
    cb = _dot_t(cm_bf, bm_bf)

    lane = lax.broadcasted_iota(I32, (c, LANES), 1)
    trow = lax.broadcasted_iota(I32, (c, LANES), 0)
    low = lane < half
    s_lo = lane % half
    mask_a, mask_b = trow >= s_lo, trow >= s_lo + half
    cb_sw = pltpu.roll(cb, half, axis=1)
    cb_a, cb_b = jnp.where(low, cb, cb_sw), jnp.where(low, cb_sw, cb)
    low1 = low[0:1]

    ys = []
    for p in range(SSD_HEADS_PER_GROUP // 2):
        h0 = g * SSD_HEADS_PER_GROUP + 2 * p
        r0, r1 = cumt_ref[pl.ds(h0, 1), :], cumt_ref[pl.ds(h0 + 1, 1), :]
        r0s, r1s = cumts_ref[pl.ds(h0, 1), :], cumts_ref[pl.ds(h0 + 1, 1), :]
        row_a, row_b = jnp.where(low1, r0, r1s), jnp.where(low1, r0s, r1)
        cp = cum_e[:, p * LANES:(p + 1) * LANES]
        w_a = jnp.exp(jnp.where(mask_a, cp - row_a, NEG)) * cb_a
        w_b = jnp.exp(jnp.where(mask_b, cp - row_b, NEG)) * cb_b
        xp = xd[:, p * LANES:(p + 1) * LANES]
        top, bot = jnp.where(low, xp, 0.0), jnp.where(low, 0.0, xp)
        rhs = jnp.concatenate([top[:half], bot[:half], top[half:], bot[half:]], axis=0).astype(BF16)
        ys.append(jnp.dot(jnp.concatenate([w_a, w_b], axis=1).astype(BF16), rhs, preferred_element_type=F32))

    h_old = h_ref[0, g]
    y = jnp.concatenate(ys, axis=1) + _dot_t(cm_bf, h_old.astype(BF16)) * jnp.exp(cum_e) + dsk_ref[...] * xs
    y = y * _silu(z_ref[...])
    y = y * lax.rsqrt(jnp.mean(y * y, axis=1, keepdims=True) + EPS)
    y_ref[...] = (y * ng_ref[...]).astype(y_ref.dtype)

    tail_x = (xd * jnp.exp(cum_e[c - 1:c] - cum_e)).astype(BF16)
    upd = _tdot(tail_x, bm_bf)
    last = cum_ref[c - 1:c, :]
    for j in range(SSD_HEADS_PER_GROUP):
        rs = slice(j * half, (j + 1) * half)
        decay = jnp.exp(_lane_col(last, g * SSD_HEADS_PER_GROUP + j))
        h_ref[0, g, rs, :] = decay * h_old[rs] + upd[rs]


def _ssd_prompt(proj, dt_raw, p, n_seq, seq_len):
    c = SSD_CHUNK
    nc = seq_len // c
    gd, st = SSD_GROUP_DIM, SSD_STATE
    xo, bo, co = SSD_INNER // gd, 2 * SSD_INNER // st, (2 * SSD_INNER + SSD_GROUPS * st) // st
    cxo, cbo, cco = 0, SSD_INNER // st, (SSD_INNER + SSD_GROUPS * st) // st

    def rows(f):
        return lambda b, k, g: (b * nc + k, f(g))

    in_specs = [
        pl.BlockSpec((c, gd), rows(lambda g: g)),
        pl.BlockSpec((c, gd), rows(lambda g: xo + g)),
        pl.BlockSpec((c, st), rows(lambda g: bo + g)),
        pl.BlockSpec((c, st), rows(lambda g: co + g)),
        pl.BlockSpec((c, LANES), rows(lambda g: 0)),
        pl.BlockSpec((4, gd), lambda b, k, g: (0, cxo + g)),
        pl.BlockSpec((4, st), lambda b, k, g: (0, cbo + g)),
        pl.BlockSpec((4, st), lambda b, k, g: (0, cco + g)),
        pl.BlockSpec((1, gd), lambda b, k, g: (0, cxo + g)),
        pl.BlockSpec((1, st), lambda b, k, g: (0, cbo + g)),
        pl.BlockSpec((1, st), lambda b, k, g: (0, cco + g)),
        pl.BlockSpec((1, LANES), lambda b, k, g: (0, 0)),
        pl.BlockSpec((1, LANES), lambda b, k, g: (0, 0)),
        pl.BlockSpec((1, gd), lambda b, k, g: (0, g)),
        pl.BlockSpec((1, gd), lambda b, k, g: (0, g)),
        pl.BlockSpec((1, 3 * LANES, gd), lambda b, k, g: (g, 0, 0)),
    ]
    head = jnp.arange(LANES, dtype=I32)[None, :, None]
    chan = jnp.arange(gd, dtype=I32)[None, None, :]
    grp = jnp.arange(SSD_GROUPS, dtype=I32)[:, None, None]
    expand = (head == grp * SSD_HEADS_PER_GROUP + chan // SSD_HEAD_DIM).astype(BF16)
    expand3 = jnp.concatenate([expand] * 3, axis=1)
    assert c == LANES
    return pl.pallas_call(
        _ssd_prompt_body, grid=(n_seq, nc, SSD_GROUPS), in_specs=in_specs,
        out_specs=[pl.BlockSpec((c, gd), lambda b, k, g: (b * nc + k, g)),
                   pl.BlockSpec((1, SSD_GROUPS, gd, st), lambda b, k, g: (b, 0, 0, 0))],
        out_shape=[jax.ShapeDtypeStruct((n_seq * seq_len, SSD_INNER), BF16),
                   jax.ShapeDtypeStruct((n_seq, SSD_GROUPS, gd, st), F32)],
        scratch_shapes=[pltpu.VMEM((SSD_GROUPS, SUBLANES, gd), F32), pltpu.VMEM((SSD_GROUPS, SUBLANES, st), F32),
                        pltpu.VMEM((SSD_GROUPS, SUBLANES, st), F32),
                        pltpu.VMEM((c, LANES), F32), pltpu.VMEM((LANES, c), F32), pltpu.VMEM((LANES, c), F32),
                        pltpu.VMEM((c, 3 * LANES), BF16), pltpu.VMEM((c, 3 * LANES), BF16)],
        compiler_params=pltpu.CompilerParams(dimension_semantics=("arbitrary",) * 3),
        name="ssd_prompt",
    )(proj, proj, proj, proj, dt_raw, p["conv_w"], p["conv_w"], p["conv_w"], p["conv_b"], p["conv_b"],
      p["conv_b"], p["dt_bias"], p["a_log"], p["d_lanes"], p["norm"], expand3)


def _expand_heads(v, first_head, n_heads):
    lane = lax.broadcasted_iota(I32, (v.shape[0], LANES), 1)
    low = lane < SSD_HEAD_DIM
    out = []
    for q in range(n_heads // 2):
        j = first_head + 2 * q
        out.append(jnp.where(low, v[:, j:j + 1], v[:, j + 1:j + 2]))
    return jnp.concatenate(out, axis=1)


def _ssd_sample_body(z_ref, xr_ref, br_ref, cr_ref, dt_ref, prevx_ref, prevb_ref, prevc_ref, cwx_ref, cwb_ref,
                     cwc_ref, cbx_ref, cbb_ref, cbc_ref, dtb_ref, alog_ref, dsk_ref, ng_ref, h0_ref,
                     y_ref, h_ref):
    n = SUBLANES
    row1 = lax.broadcasted_iota(I32, (n, 1), 0)
    pos1, first1 = row1 % SAMPLE_LEN, row1 < SAMPLE_LEN

    def src(v, s):
        return jnp.where(first1, v[s:s + 1], v[SAMPLE_LEN + s:SAMPLE_LEN + s + 1])

    xs = _conv_silu_sample(xr_ref[...], prevx_ref[...], cwx_ref[...], cbx_ref[...], pos1)
    bm = _conv_silu_sample(br_ref[...], prevb_ref[...], cwb_ref[...], cbb_ref[...], pos1)
    cm = _conv_silu_sample(cr_ref[...], prevc_ref[...], cwc_ref[...], cbc_ref[...], pos1)
    dt = _softplus(dt_ref[...] + dtb_ref[...])
    la = dt * (-jnp.exp(alog_ref[...]))
    cum = la + jnp.where(pos1 >= 1, pltpu.roll(la, 1, axis=0), 0.0)
    cum = cum + jnp.where(pos1 >= 2, pltpu.roll(cum, 2, axis=0), 0.0)
    z = z_ref[...]
    pad_rows = LANES - n
    hrow = lax.broadcasted_iota(I32, (SSD_GROUP_DIM, SSD_STATE), 0) // SSD_HEAD_DIM

    ys = []
    for g in range(SSD_GROUPS):
        gs = slice(g * SSD_GROUP_DIM, (g + 1) * SSD_GROUP_DIM)
        ns = slice(g * SSD_STATE, (g + 1) * SSD_STATE)
        x_g, b_g, c_g = xs[:, gs], bm[:, ns], cm[:, ns]
        cum_e = _expand_heads(cum, g * SSD_HEADS_PER_GROUP, SSD_HEADS_PER_GROUP)
        dt_e = _expand_heads(dt, g * SSD_HEADS_PER_GROUP, SSD_HEADS_PER_GROUP)
        last_e = src(cum_e, SAMPLE_LEN - 1)
        y = jnp.zeros((n, SSD_GROUP_DIM), F32)
        for s in range(SAMPLE_LEN):
            cb_s = jnp.sum(c_g * src(b_g, s), axis=1, keepdims=True)
            decay = jnp.exp(jnp.where(pos1 >= s, cum_e - src(cum_e, s), NEG))
            y = y + decay * cb_s * src(dt_e, s) * src(x_g, s)
        c_bf = c_g.astype(BF16)
        h_old = [h0_ref[q, g] for q in range(SEQS_PER_STEP)]
        y_inter = jnp.where(first1, _dot_t(c_bf, h_old[0].astype(BF16)), _dot_t(c_bf, h_old[1].astype(BF16)))
        y = y + y_inter * jnp.exp(cum_e) + dsk_ref[:, gs] * x_g
        y = y * _silu(z[:, gs])
        y = y * lax.rsqrt(jnp.mean(y * y, axis=1, keepdims=True) + EPS)
        ys.append(y * ng_ref[:, gs])
        xw = x_g * jnp.exp(last_e - cum_e) * dt_e
        b_pad = jnp.concatenate([b_g, jnp.zeros((pad_rows, SSD_STATE), F32)], axis=0).astype(BF16)
        for q in range(SEQS_PER_STEP):
            own = (row1 >= q * SAMPLE_LEN) & (row1 < (q + 1) * SAMPLE_LEN)
            xw_pad = jnp.concatenate([jnp.where(own, xw, 0.0), jnp.zeros((pad_rows, SSD_GROUP_DIM), F32)],
                                     axis=0).astype(BF16)
            last_row = (q + 1) * SAMPLE_LEN - 1
            elast = jnp.exp(cum[last_row:last_row + 1])
            dec = jnp.zeros((SSD_GROUP_DIM, SSD_STATE), F32)
            for j in range(SSD_HEADS_PER_GROUP):
                hd = g * SSD_HEADS_PER_GROUP + j
                dec = jnp.where(hrow == j, elast[:, hd:hd + 1], dec)
            h_ref[q, g] = dec * h_old[q] + _tdot(xw_pad, b_pad)
    y_ref[...] = jnp.concatenate(ys, axis=1).astype(y_ref.dtype)


def _ssd_sample(proj, dt_raw, prev, h0, p, row0, n_seq):
    n = SUBLANES
    steps = n_seq // SEQS_PER_STEP
    r0 = row0 // n
    st = SSD_STATE
    zo, xo, bo, co = 0, 1, 2 * SSD_INNER // (SSD_GROUPS * st), (2 * SSD_INNER + SSD_GROUPS * st) // (SSD_GROUPS * st)
    gw = SSD_GROUPS * st
    in_specs = [
        pl.BlockSpec((n, SSD_INNER), lambda i: (r0 + i, zo)),
        pl.BlockSpec((n, SSD_INNER), lambda i: (r0 + i, xo)),
        pl.BlockSpec((n, gw), lambda i: (r0 + i, bo)),
        pl.BlockSpec((n, gw), lambda i: (r0 + i, co)),
        pl.BlockSpec((n, LANES), lambda i: (r0 + i, 0)),
        pl.BlockSpec((n, SSD_INNER), lambda i: (i, 0)),
        pl.BlockSpec((n, gw), lambda i: (i, SSD_INNER // gw)),
        pl.BlockSpec((n, gw), lambda i: (i, SSD_INNER // gw + 1)),
        pl.BlockSpec((4, SSD_INNER), lambda i: (0, 0)),
        pl.BlockSpec((4, gw), lambda i: (0, SSD_INNER // gw)),
        pl.BlockSpec((4, gw), lambda i: (0, SSD_INNER // gw + 1)),
        pl.BlockSpec((1, SSD_INNER), lambda i: (0, 0)),
        pl.BlockSpec((1, gw), lambda i: (0, SSD_INNER // gw)),
        pl.BlockSpec((1, gw), lambda i: (0, SSD_INNER // gw + 1)),
        pl.BlockSpec((1, LANES), lambda i: (0, 0)),
        pl.BlockSpec((1, LANES), lambda i: (0, 0)),
        pl.BlockSpec((1, SSD_INNER), lambda i: (0, 0)),
        pl.BlockSpec((1, SSD_INNER), lambda i: (0, 0)),
        pl.BlockSpec((SEQS_PER_STEP, SSD_GROUPS, SSD_GROUP_DIM, st), lambda i: (i, 0, 0, 0)),
    ]
    state_bytes = SEQS_PER_STEP * SSD_INNER * st * 4
    return pl.pallas_call(
        _ssd_sample_body, grid=(steps,), in_specs=in_specs,
        out_specs=[pl.BlockSpec((n, SSD_INNER), lambda i: (i, 0)),
                   pl.BlockSpec((SEQS_PER_STEP, SSD_GROUPS, SSD_GROUP_DIM, st), lambda i: (i, 0, 0, 0))],
        out_shape=[jax.ShapeDtypeStruct((n_seq * SAMPLE_LEN, SSD_INNER), BF16),
                   jax.ShapeDtypeStruct((n_seq, SSD_GROUPS, SSD_GROUP_DIM, st), F32)],
        compiler_params=_vmem_params(4 * state_bytes + (8 << 20), ("arbitrary",)),
        name="ssd_sample",
    )(proj, proj, proj, proj, dt_raw, prev, prev, prev, p["conv_w"], p["conv_w"], p["conv_w"], p["conv_b"],
      p["conv_b"], p["conv_b"], p["dt_bias"], p["a_log"], p["d_lanes"], p["norm"], h0)


def _ml_prompt_body(xc_ref, v_ref, o_ref, gt_ref, gb_ref, wq_ref, wk_ref, cw_ref, cb_ref, ng_ref,
                    y_ref, cs_ref, ns_ref, ms_ref, px_ref):
    c = xc_ref.shape[0]
    k, h = pl.program_id(1), pl.program_id(2)

    @pl.when(k == 0)
    def _init():
        cs_ref[0, h] = jnp.zeros((ML_V_DIM, ML_QK_DIM), F32)
        ns_ref[0, h] = jnp.zeros((1, ML_QK_DIM), F32)
        ms_ref[0, h] = jnp.zeros((1, LANES), F32)
        px_ref[h] = jnp.zeros((SUBLANES, ML_V_DIM), F32)

    u = xc_ref[...]
    xb = _conv_silu(u, px_ref[h], cw_ref[...], cb_ref[...]).astype(BF16)
    px_ref[h] = u[c - SUBLANES:]
    q = jnp.dot(xb, wq_ref[0], preferred_element_type=F32) * (ML_QK_DIM ** -0.5)
    kk = jnp.dot(xb, wk_ref[0], preferred_element_type=F32)
    q_bf, k_bf = q.astype(BF16), kk.astype(BF16)
    v = v_ref[...]

    gates = gt_ref[...] + gb_ref[...]
    lane = lax.broadcasted_iota(I32, gates.shape, 1)
    log_f = jnp.where((lane >= ML_HEADS) & (lane < 2 * ML_HEADS), -_softplus(-gates), 0.0)
    ig = _lane_col(gates, h)
    bcol = _lane_col(_cumsum_rows(log_f), ML_HEADS + h)

    row = lax.broadcasted_iota(I32, (c, c), 0)
    col = lax.broadcasted_iota(I32, (c, c), 1)
    eye, causal = row == col, col <= row
    logd = jnp.where(causal, bcol - _col_to_row(bcol, eye) + _col_to_row(ig, eye), NEG)
    m_prev = ms_ref[0, h][:, 0:1]
    inter = bcol + m_prev
    mt = jnp.maximum(jnp.max(logd, axis=1, keepdims=True), inter)
    s = _dot_t(q_bf, k_bf) * jnp.exp(logd - mt)
    wi = jnp.exp(inter - mt)
    c_old, n_old = cs_ref[0, h], ns_ref[0, h]
    num = jnp.dot(s.astype(BF16), v.astype(BF16), preferred_element_type=F32) + wi * _dot_t(q_bf, c_old.astype(BF16))
    den = jnp.sum(s, axis=1, keepdims=True) + wi * jnp.sum(q * n_old, axis=1, keepdims=True)
    hout = num / jnp.maximum(jnp.abs(den), jnp.exp(-mt))
    hout = hout * lax.rsqrt(jnp.mean(hout * hout, axis=1, keepdims=True) + EPS)
    y_ref[...] = (hout * ng_ref[...] * jax.nn.sigmoid(o_ref[...])).astype(y_ref.dtype)

    blast = bcol[c - 1:c]
    gcol = blast - bcol + ig
    carry = blast + m_prev
    m_new = jnp.maximum(carry, jnp.max(gcol, axis=0, keepdims=True))
    w_s, w_c = jnp.exp(gcol - m_new), jnp.exp(carry - m_new)
    cs_ref[0, h] = w_c * c_old + _tdot((v * w_s).astype(BF16), k_bf)
    ns_ref[0, h] = w_c * n_old + jnp.sum(w_s * kk, axis=0, keepdims=True)
    ms_ref[0, h] = jnp.broadcast_to(m_new, (1, LANES))


def _ml_prompt(proj, gates, p, n_seq, seq_len):
    c = ML_CHUNK
    nc = seq_len // c
    dv, dk = ML_V_DIM, ML_QK_DIM
    in_specs = [
        pl.BlockSpec((c, dv), lambda b, k, h: (b * nc + k, h)),
        pl.BlockSpec((c, dv), lambda b, k, h: (b * nc + k, ML_HEADS + h)),
        pl.BlockSpec((c, dv), lambda b, k, h: (b * nc + k, 2 * ML_HEADS + h)),
        pl.BlockSpec((c, LANES), lambda b, k, h: (b * nc + k, 0)),
        pl.BlockSpec((1, LANES), lambda b, k, h: (0, 0)),
        pl.BlockSpec((1, dv, dk), lambda b, k, h: (h, 0, 0)),
        pl.BlockSpec((1, dv, dk), lambda b, k, h: (h, 0, 0)),
        pl.BlockSpec((4, dv), lambda b, k, h: (0, h)),
        pl.BlockSpec((1, dv), lambda b, k, h: (0, h)),
        pl.BlockSpec((1, dv), lambda b, k, h: (0, h)),
    ]
    return pl.pallas_call(
        _ml_prompt_body, grid=(n_seq, nc, ML_HEADS), in_specs=in_specs,
        out_specs=[pl.BlockSpec((c, dv), lambda b, k, h: (b * nc + k, h)),
                   pl.BlockSpec((1, ML_HEADS, dv, dk), lambda b, k, h: (b, 0, 0, 0)),
                   pl.BlockSpec((1, ML_HEADS, 1, dk), lambda b, k, h: (b, 0, 0, 0)),
                   pl.BlockSpec((1, ML_HEADS, 1, LANES), lambda b, k, h: (b, 0, 0, 0))],
        out_shape=[jax.ShapeDtypeStruct((n_seq * seq_len, ML_INNER), BF16),
                   jax.ShapeDtypeStruct((n_seq, ML_HEADS, dv, dk), F32),
                   jax.ShapeDtypeStruct((n_seq, ML_HEADS, 1, dk), F32),
                   jax.ShapeDtypeStruct((n_seq, ML_HEADS, 1, LANES), F32)],
        scratch_shapes=[pltpu.VMEM((ML_HEADS, SUBLANES, dv), F32)],
        compiler_params=_vmem_params(4 * ML_HEADS * dv * dk * 4 + (8 << 20), ("arbitrary",) * 3),
        name="mlstm_prompt",
    )(proj, proj, proj, gates, p["gate_bias"], p["wq"], p["wk"], p["conv_w"], p["conv_b"], p["norm"])


def _ml_sample_body(xc_ref, v_ref, o_ref, gt_ref, prev_ref, gb_ref, wq_ref, wk_ref, cw_ref, cb_ref, ng_ref,
                    c0_ref, n0_ref, m0_ref, y_ref, cs_ref, ns_ref, ms_ref):
    n = SUBLANES
    row1 = lax.broadcasted_iota(I32, (n, 1), 0)
    pos1, first1 = row1 % SAMPLE_LEN, row1 < SAMPLE_LEN

    def src(v, s):
        return jnp.where(first1, v[s:s + 1], v[SAMPLE_LEN + s:SAMPLE_LEN + s + 1])

    def per_seq(vals):
        return jnp.where(first1, vals[0], vals[1])

    xs = _conv_silu_sample(xc_ref[...], prev_ref[...], cw_ref[...], cb_ref[...], pos1)
    gates = gt_ref[...] + gb_ref[...]
    lane = lax.broadcasted_iota(I32, gates.shape, 1)
    log_f = jnp.where((lane >= ML_HEADS) & (lane < 2 * ML_HEADS), -_softplus(-gates), 0.0)
    bsum = log_f + jnp.where(pos1 >= 1, pltpu.roll(log_f, 1, axis=0), 0.0)
    bsum = bsum + jnp.where(pos1 >= 2, pltpu.roll(bsum, 2, axis=0), 0.0)
    v_all, o_all = v_ref[...], o_ref[...]
    pad_rows = LANES - n

    ys = []
    for h in range(ML_HEADS):
        hs = slice(h * ML_V_DIM, (h + 1) * ML_V_DIM)
        xb = xs[:, hs].astype(BF16)
        q = jnp.dot(xb, wq_ref[h], preferred_element_type=F32) * (ML_QK_DIM ** -0.5)
        kk = jnp.dot(xb, wk_ref[h], preferred_element_type=F32)
        v = v_all[:, hs]
        ig = gates[:, h:h + 1]
        bcol = bsum[:, ML_HEADS + h:ML_HEADS + h + 1]
        c_old = [c0_ref[s_, h] for s_ in range(SEQS_PER_STEP)]
        n_old = [n0_ref[s_, h] for s_ in range(SEQS_PER_STEP)]
        m_old = [m0_ref[s_, h][:, 0:1] for s_ in range(SEQS_PER_STEP)]
        inter = bcol + per_seq(m_old)
        logd = [jnp.where(pos1 >= s, bcol - src(bcol, s) + src(ig, s), NEG) for s in range(SAMPLE_LEN)]
        mt = inter
        for ld in logd:
            mt = jnp.maximum(mt, ld)
        wi = jnp.exp(inter - mt)
        q_bf = q.astype(BF16)
        num = wi * per_seq([_dot_t(q_bf, cq.astype(BF16)) for cq in c_old])
        den = wi * jnp.sum(q * per_seq(n_old), axis=1, keepdims=True)
        for s in range(SAMPLE_LEN):
            sw = jnp.sum(q * src(kk, s), axis=1, keepdims=True) * jnp.exp(logd[s] - mt)
            num = num + sw * src(v, s)
            den = den + sw
        hout = num / jnp.maximum(jnp.abs(den), jnp.exp(-mt))
        hout = hout * lax.rsqrt(jnp.mean(hout * hout, axis=1, keepdims=True) + EPS)
        ys.append(hout * ng_ref[:, hs] * jax.nn.sigmoid(o_all[:, hs]))

        blast = src(bcol, SAMPLE_LEN - 1)
        gcol = blast - bcol + ig
        k_pad = jnp.concatenate([kk, jnp.zeros((pad_rows, ML_QK_DIM), F32)], axis=0).astype(BF16)
        for s_ in range(SEQS_PER_STEP):
            own = (row1 >= s_ * SAMPLE_LEN) & (row1 < (s_ + 1) * SAMPLE_LEN)
            last_row = (s_ + 1) * SAMPLE_LEN - 1
            carry = bcol[last_row:last_row + 1] + m_old[s_]
            m_new = jnp.maximum(carry, jnp.max(jnp.where(own, gcol, NEG), axis=0, keepdims=True))
            w_s = jnp.where(own, jnp.exp(gcol - m_new), 0.0)
            w_c = jnp.exp(carry - m_new)
            vw_pad = jnp.concatenate([v * w_s, jnp.zeros((pad_rows, ML_V_DIM), F32)], axis=0).astype(BF16)
            cs_ref[s_, h] = w_c * c_old[s_] + _tdot(vw_pad, k_pad)
            ns_ref[s_, h] = w_c * n_old[s_] + jnp.sum(w_s * kk, axis=0, keepdims=True)
            ms_ref[s_, h] = jnp.broadcast_to(m_new, (1, LANES))
    y_ref[...] = jnp.concatenate(ys, axis=1).astype(y_ref.dtype)


def _ml_sample(proj, gates, prev, c0, n0, m0, p, row0, n_seq):
    n = SUBLANES
    steps = n_seq // SEQS_PER_STEP
    r0 = row0 // n
    dv, dk = ML_V_DIM, ML_QK_DIM
    sq = SEQS_PER_STEP
    in_specs = [
        pl.BlockSpec((n, ML_INNER), lambda i: (r0 + i, 0)),
        pl.BlockSpec((n, ML_INNER), lambda i: (r0 + i, 1)),
        pl.BlockSpec((n, ML_INNER), lambda i: (r0 + i, 2)),
        pl.BlockSpec((n, LANES), lambda i: (r0 + i, 0)),
        pl.BlockSpec((n, ML_INNER), lambda i: (i, 0)),
        pl.BlockSpec((1, LANES), lambda i: (0, 0)),
        pl.BlockSpec((ML_HEADS, dv, dk), lambda i: (0, 0, 0)),
        pl.BlockSpec((ML_HEADS, dv, dk), lambda i: (0, 0, 0)),
        pl.BlockSpec((4, ML_INNER), lambda i: (0, 0)),
        pl.BlockSpec((1, ML_INNER), lambda i: (0, 0)),
        pl.BlockSpec((1, ML_INNER), lambda i: (0, 0)),
        pl.BlockSpec((sq, ML_HEADS, dv, dk), lambda i: (i, 0, 0, 0)),
        pl.BlockSpec((sq, ML_HEADS, 1, dk), lambda i: (i, 0, 0, 0)),
        pl.BlockSpec((sq, ML_HEADS, 1, LANES), lambda i: (i, 0, 0, 0)),
    ]
    state_bytes = sq * ML_HEADS * dv * dk * 4
    return pl.pallas_call(
        _ml_sample_body, grid=(steps,), in_specs=in_specs,
        out_specs=[pl.BlockSpec((n, ML_INNER), lambda i: (i, 0)),
                   pl.BlockSpec((sq, ML_HEADS, dv, dk), lambda i: (i, 0, 0, 0)),
                   pl.BlockSpec((sq, ML_HEADS, 1, dk), lambda i: (i, 0, 0, 0)),
                   pl.BlockSpec((sq, ML_HEADS, 1, LANES), lambda i: (i, 0, 0, 0))],
        out_shape=[jax.ShapeDtypeStruct((n_seq * SAMPLE_LEN, ML_INNER), BF16),
                   jax.ShapeDtypeStruct((n_seq, ML_HEADS, dv, dk), F32),
                   jax.ShapeDtypeStruct((n_seq, ML_HEADS, 1, dk), F32),
                   jax.ShapeDtypeStruct((n_seq, ML_HEADS, 1, LANES), F32)],
        compiler_params=_vmem_params(4 * state_bytes + (12 << 20), ("arbitrary",)),
        name="mlstm_sample",
    )(proj, proj, proj, gates, prev, p["gate_bias"], p["wq"], p["wk"], p["conv_w"], p["conv_b"], p["norm"],
      c0, n0, m0)


def _routing_tables(top_e, n_blocks):
    n_assign = top_e.size
    e_flat = top_e.reshape(-1)
    onehot = (e_flat[:, None] == jnp.arange(N_EXPERTS, dtype=I32)[None, :]).astype(I32)
    csum = jnp.cumsum(onehot, axis=0)
    rank = jnp.sum((csum - onehot) * onehot, axis=1)
    counts = csum[-1]
    padded = (counts + MOE_ROWS - 1) // MOE_ROWS * MOE_ROWS
    pends = jnp.cumsum(padded)
    pos = jnp.sum(onehot * (pends - padded)[None, :], axis=1) + rank
    n_rows = n_blocks * MOE_ROWS
    slot_tok = jnp.zeros((n_rows,), I32).at[pos].set(jnp.arange(n_assign, dtype=I32) // TOP_K)
    block_start = jnp.arange(n_blocks, dtype=I32) * MOE_ROWS
    block_e = jnp.minimum(jnp.sum((pends[None, :] <= block_start[:, None]).astype(I32), axis=1), N_EXPERTS - 1)
    recast = jnp.concatenate([jnp.ones((1,), I32), (block_e[1:] != block_e[:-1]).astype(I32)])
    n_used = (pends[-1:] // MOE_ROWS).astype(I32)
    return pos, slot_tok, block_e, recast, n_used


def kernel(x_prompt, x_sample, state_ssm, state_ssm_conv, state_mlstm_c, state_mlstm_n, state_mlstm_m, state_mlstm_conv, norm_mix_a, ssd_w_in, ssd_conv_w, ssd_conv_b, ssd_dt_bias, ssd_a_log, ssd_d, ssd_norm, ssd_w_out, norm_ffn_a, ffn_w_gate, ffn_w_up, ffn_w_down, norm_mix_b, ml_w_in, ml_conv_w, ml_conv_b, ml_w_q, ml_w_k, ml_b_i, ml_b_f, ml_norm, ml_w_out, norm_ffn_b, moe_w_router, moe_b_router, moe_w_gate, moe_w_up, moe_w_down, final_norm):
    bp, lp, d = x_prompt.shape
    bs, ls, _ = x_sample.shape
    assert ls == SAMPLE_LEN and lp % ML_CHUNK == 0 and lp % SSD_CHUNK == 0 and bs % SEQS_PER_STEP == 0
    tp, ts = bp * lp, bs * ls
    t = tp + ts
    assert tp % ROW_TILE == 0 and ts % ROW_TILE == 0
    tall = t // 8
    assert t % (8 * 32) == 0
    x = jnp.concatenate([x_prompt.reshape(tp, d), x_sample.reshape(ts, d)], axis=0)

    def pad_lanes(v):
        return jnp.pad(v.reshape(1, -1), ((0, 0), (0, LANES - v.size)))

    def prev_rows(state):
        return jnp.pad(state, ((0, 0), (1, 0), (0, 0))).reshape(state.shape[0] * SAMPLE_LEN, state.shape[2])

    z_cols = SSD_INNER + SSD_XBC
    def last_rows(proj, lo, hi):
        rows_p = jnp.stack([proj[(b + 1) * lp - 3:(b + 1) * lp, lo:hi] for b in range(bp)])
        rows_s = proj[tp:, lo:hi].reshape(bs, ls, hi - lo)[:, ls - 3:]
        return rows_p, rows_s

    ssd_w_in_nk = jnp.swapaxes(ssd_w_in, 1, 2)
    xn, dt_raw = _rmsnorm_narrow_matmul(x, norm_mix_a[0], ssd_w_in_nk, z_cols, "ssd_norm_dt")
    proj = _matmul(xn, [ssd_w_in_nk], tm=tall, tn=1024, n_cols=z_cols, out_dtype=F32, name="ssd_in", w_nk=True)
    ssd_p = dict(conv_w=ssd_conv_w[0], conv_b=ssd_conv_b[0].reshape(1, -1), dt_bias=pad_lanes(ssd_dt_bias[0]),
                 a_log=pad_lanes(ssd_a_log[0]), d_lanes=jnp.repeat(ssd_d[0], SSD_HEAD_DIM).reshape(1, -1),
                 norm=ssd_norm[0].reshape(1, -1))
    y_p, ssm_p = _ssd_prompt(proj, dt_raw, ssd_p, bp, lp)
    y_s, ssm_s = _ssd_sample(proj, dt_raw, prev_rows(state_ssm_conv[0]),
                             state_ssm[0].reshape(bs, SSD_GROUPS, SSD_GROUP_DIM, SSD_STATE), ssd_p, tp, bs)
    ssm_conv_p, ssm_conv_s = last_rows(proj, SSD_INNER, z_cols)
    x = _matmul(y_p, [ssd_w_out], a_tail=y_s, tm=ROW_TILE, tn=512, n_cols=d, out_dtype=F32, name="ssd_out", res=x)

    xn = _rmsnorm(x, norm_ffn_a[0], BF16)
    hid = _matmul(xn, [ffn_w_gate, ffn_w_up], tm=2 * tall, tn=512, n_cols=ffn_w_gate.shape[2],
                  out_dtype=BF16, name="ffn_gate_up")
    x = _matmul(hid, [ffn_w_down], tm=tall // 2, tn=512, n_cols=d, out_dtype=F32, name="ffn_down", res=x)

    ml_cols = 3 * ML_INNER
    ml_w_in_nk = jnp.swapaxes(ml_w_in, 1, 2)
    xn, gates = _rmsnorm_narrow_matmul(x, norm_mix_b[0], ml_w_in_nk, ml_cols, "ml_norm_gates")
    proj2 = _matmul(xn, [ml_w_in_nk], tm=tall, tn=1024, n_cols=ml_cols, out_dtype=F32, name="ml_in", w_nk=True)
    ml_p = dict(conv_w=ml_conv_w[0], conv_b=ml_conv_b[0].reshape(1, -1),
                gate_bias=pad_lanes(jnp.concatenate([ml_b_i[0], ml_b_f[0]])),
                wq=ml_w_q[0].astype(BF16), wk=ml_w_k[0].astype(BF16), norm=ml_norm[0].reshape(1, -1))
    y_p, c_p, n_p, m_p = _ml_prompt(proj2, gates, ml_p, bp, lp)
    y_s, c_s, n_s, m_s = _ml_sample(
        proj2, gates, prev_rows(state_mlstm_conv[0]), state_mlstm_c[0],
        state_mlstm_n[0].reshape(bs, ML_HEADS, 1, ML_QK_DIM),
        jnp.broadcast_to(state_mlstm_m[0][:, :, None, None], (bs, ML_HEADS, 1, LANES)), ml_p, tp, bs)
    ml_conv_p, ml_conv_s = last_rows(proj2, 0, ML_INNER)
    x = _matmul(y_p, [ml_w_out], a_tail=y_s, tm=ROW_TILE, tn=512, n_cols=d, out_dtype=F32, name="ml_out", res=x)

    top_e, top_g = _router(x, norm_ffn_b[0], moe_w_router[0], moe_b_router[0])
    n_blocks = -(-(t * TOP_K + N_EXPERTS * (MOE_ROWS - 1)) // MOE_ROWS)
    pos, slot_tok, block_e, recast, n_used = _routing_tables(top_e, n_blocks)
    xg = _gather_norm_rows(x, norm_ffn_b[0], slot_tok, n_used, MOE_ROWS)
    route = dict(block_e=block_e, recast=recast, n_used=n_used)
    hid = _matmul(xg, [moe_w_gate[0], moe_w_up[0]], tm=MOE_ROWS, tn=1024, n_cols=moe_w_gate.shape[3],
                  out_dtype=BF16, name="moe_gate_up", **route)
    yb = _matmul(hid, [moe_w_down[0]], tm=MOE_ROWS, tn=512, n_cols=d, out_dtype=F32, name="moe_down", **route)
    out_p, out_s = _combine(yb, pos, top_g, x, final_norm, tp)

    return (out_p.reshape(bp, lp, d), out_s.reshape(bs, ls, d),
            ssm_p.reshape(1, bp, *state_ssm.shape[2:]), ssm_conv_p[None],
            c_p[None], n_p.reshape(1, bp, ML_HEADS, ML_QK_DIM), m_p[:, :, 0, 0][None], ml_conv_p[None],
            ssm_s.reshape(1, bs, *state_ssm.shape[2:]), ssm_conv_s[None],
            c_s[None], n_s.reshape(1, bs, ML_HEADS, ML_QK_DIM), m_s[:, :, 0, 0][None], ml_conv_s[None])
```
